```python
import jax
import jax.numpy as jnp
from jax import lax
import numpy as np

D_MODEL = 2048
BATCH = 4
SEQ = 2048
DEPTH = 4
DEC_BATCH = 128
DEC_SEQ = 8
PAST_LEN = 8192
PAGE_SIZE = 128

A_WIDTH = 1024
A_GROUPS = 8
A_CHUNK = 128
B_HEADS = 16
B_DK = 128
B_DV = 64
B_WIDTH = B_HEADS * B_DV
CMP_LEN = 32
CMP_STRIDE = 16
SEL_BLOCK = 64
N_SELECT = 16
WINDOW = 512
FORCE_BONUS = 1e4
C_HEADS = 8
C_Q_LORA = 512
C_KV_LORA = 128
C_NOPE = 128
C_ROPE = 64
C_VDIM = 128
C_WIDTH = C_HEADS * C_VDIM
ROPE_THETA = 10000.0
Q_BLOCK = 128
NORM_EPS = 1e-6

IN_WIDTHS = (
    2 * A_WIDTH, A_WIDTH,
    B_HEADS * B_DK, B_DK, B_DV, B_DK, B_DV, B_DK, B_DV, 3 * B_HEADS, B_WIDTH,
    C_Q_LORA, C_KV_LORA + C_ROPE, C_WIDTH,
    3 * D_MODEL,
)
N_IN = sum(IN_WIDTHS)
SPLIT_AT = tuple(int(s) for s in np.cumsum(IN_WIDTHS)[:-1])

kernel_name = 'hybrid_gmlp_nsa_mla_gated_decode_step'


def rmsnorm(x, g):
    xf = x.astype(jnp.float32)
    y = xf * lax.rsqrt(jnp.mean(xf * xf, axis=-1, keepdims=True) + NORM_EPS)
    return (y * g.astype(jnp.float32)).astype(x.dtype)


def layernorm(x, g, b):
    xf = x.astype(jnp.float32)
    mu = jnp.mean(xf, axis=-1, keepdims=True)
    var = jnp.mean(jnp.square(xf - mu), axis=-1, keepdims=True)
    return ((xf - mu) * lax.rsqrt(var + NORM_EPS) * g.astype(jnp.float32) + b.astype(jnp.float32)).astype(x.dtype)


def masked_softmax(s, mask):
    s = jnp.where(mask, s.astype(jnp.float32), -jnp.inf)
    m = jnp.max(s, axis=-1, keepdims=True)
    m = jnp.where(jnp.isfinite(m), m, 0.0)
    e = jnp.where(mask, jnp.exp(s - m), 0.0)
    return e / jnp.maximum(jnp.sum(e, axis=-1, keepdims=True), 1e-30)


def rope(x, pos):
    half = x.shape[-1] // 2
    inv = ROPE_THETA ** (-jnp.arange(half, dtype=jnp.float32) / half)
    ang = pos.astype(jnp.float32)[:, None] * inv
    ang = ang.reshape(ang.shape[:1] + (1,) * (x.ndim - 3) + ang.shape[1:])
    cos, sin = jnp.cos(ang), jnp.sin(ang)
    xf = x.astype(jnp.float32)
    x1, x2 = xf[..., :half], xf[..., half:]
    return jnp.concatenate([x1 * cos - x2 * sin, x1 * sin + x2 * cos], axis=-1).astype(x.dtype)


def to_blocks(x, qb):
    b, t = x.shape[:2]
    return jnp.moveaxis(x.reshape((b, t // qb, qb) + x.shape[2:]), 1, 0)


def from_blocks(y):
    y = jnp.moveaxis(y, 0, 1)
    return y.reshape((y.shape[0], y.shape[1] * y.shape[2]) + y.shape[3:])


def pad_rows(x, m):
    l = x.shape[1]
    lp = -(-l // m) * m
    return jnp.pad(x, ((0, 0), (0, lp - l), (0, 0)))


def gather_pages(pool, page_table, l):
    rows = pool[page_table, l]
    return rows.reshape(rows.shape[0], -1, rows.shape[-1])


def chunk_mlp(u, v, z, ln_g, ln_b, ws, bs):
    b, t, _ = u.shape
    c = min(t, A_CHUNK)
    vn = layernorm(v, ln_g, ln_b)
    w = jnp.where(jnp.tril(jnp.ones((c, c), dtype=bool)), ws[:, :c, :c], 0.0).astype(vn.dtype)
    vb = vn.reshape(b, t // c, c, A_GROUPS, A_WIDTH // A_GROUPS)
    s = jnp.einsum('gts,bnsgc->bntgc', w, vb) + jnp.transpose(bs[:, :c]).astype(vn.dtype)[None, None, :, :, None]
    return u * s.reshape(b, t, A_WIDTH) * jax.nn.silu(z), vn


def nsa_compress(x, pe, w1, w2):
    b, _, d = x.shape
    ch = pad_rows(x, CMP_STRIDE)
    nch = ch.shape[1] // CMP_STRIDE
    ch = ch.reshape(b, nch, CMP_STRIDE, d)
    n_sub = CMP_LEN // CMP_STRIDE
    blocks = jnp.concatenate([ch[:, j:nch - n_sub + 1 + j] for j in range(n_sub)], axis=2) + pe
    hid = jax.nn.silu(jnp.einsum('bnld,lde->bne', blocks, w1))
    return hid @ w2


def nsa_summaries(cmp_rows, p):
    kc = nsa_compress(cmp_rows[..., :B_DK], p['pe_k'], p['w1_k'], p['w2_k'])
    vc = nsa_compress(cmp_rows[..., B_DK:], p['pe_v'], p['w1_v'], p['w2_v'])
    return kc, vc


def nsa_attend(q, g, q_pos, kc, vc, slc, win, w_pos):
    scale = B_DK ** -0.5
    nc = kc.shape[1]
    c_end = jnp.arange(nc) * CMP_STRIDE + (CMP_LEN - 1)
    c_mask = (c_end[None, :] <= q_pos[:, None])[None, :, None, :]
    p_cmp = masked_softmax(jnp.einsum('bthd,bnd->bthn', q, kc) * scale, c_mask)
    o_cmp = jnp.einsum('bthn,bnd->bthd', p_cmp.astype(vc.dtype), vc)
    ns = slc.shape[1] // SEL_BLOCK
    ci = np.arange(nc)[:, None]
    sj = np.arange(ns)[None, :]
    cover = ((ci * CMP_STRIDE < (sj + 1) * SEL_BLOCK) & (ci * CMP_STRIDE + CMP_LEN > sj * SEL_BLOCK)).astype(np.float32)
    imp = jnp.einsum('btn,ns->bts', jnp.sum(p_cmp, axis=2), jnp.asarray(cover))
    jj = jnp.arange(ns)[None, :]
    cur = (q_pos // SEL_BLOCK)[:, None]
    valid = jj * SEL_BLOCK <= q_pos[:, None]
    forced = (jj == 0) | (jj == cur) | (jj == cur - 1)
    score = jnp.where(valid, imp + jnp.where(forced, FORCE_BONUS, 0.0), -jnp.inf)
    top_s, top_i = lax.top_k(score, min(N_SELECT, ns))
    tok = (top_i[..., None] * SEL_BLOCK + jnp.arange(SEL_BLOCK)).reshape(top_i.shape[:2] + (-1,))
    sel_ok = jnp.repeat(jnp.isfinite(top_s), SEL_BLOCK, axis=-1) & (tok <= q_pos[None, :, None])
    rows = jax.vmap(lambda r, i: r[i])(slc, tok)
    p_sel = masked_softmax(jnp.einsum('bthd,btkd->bthk', q, rows[..., :B_DK]) * scale, sel_ok[:, :, None, :])
    o_sel = jnp.einsum('bthk,btkd->bthd', p_sel.astype(rows.dtype), rows[..., B_DK:])
    w_mask = ((w_pos[None, :] <= q_pos[:, None]) & (w_pos[None, :] >= q_pos[:, None] - WINDOW)
              & (w_pos[None, :] >= 0))[None, :, None, :]
    p_w = masked_softmax(jnp.einsum('bthd,bkd->bthk', q, win[..., :B_DK]) * scale, w_mask)
    o_w = jnp.einsum('bthk,bkd->bthd', p_w.astype(win.dtype), win[..., B_DK:])
    gs = jax.nn.sigmoid(g.astype(jnp.float32)).astype(q.dtype)
    return gs[..., 0:1] * o_cmp + gs[..., 1:2] * o_sel + gs[..., 2:3] * o_w


def nsa_prompt(q, g, cmp_rows, slc_rows, win_rows, p):
    t = q.shape[1]
    qb = min(Q_BLOCK, t)
    kc, vc = nsa_summaries(cmp_rows, p)
    slc = pad_rows(slc_rows, SEL_BLOCK)
    win = jnp.pad(win_rows, ((0, 0), (WINDOW, 0), (0, 0)))

    def block(args):
        q_b, g_b, i = args
        start = i * qb
        q_pos = start + jnp.arange(qb)
        w_b = lax.dynamic_slice_in_dim(win, start, qb + WINDOW, axis=1)
        w_pos = start - WINDOW + jnp.arange(qb + WINDOW)
        return nsa_attend(q_b, g_b, q_pos, kc, vc, slc, w_b, w_pos)

    return from_blocks(lax.map(block, (to_blocks(q, qb), to_blocks(g, qb), jnp.arange(t // qb))))


def nsa_sample(q, g, cmp_rows, slc_rows, win_rows, p, past_cmp, past_slc, win_buf, past_len):
    t = q.shape[1]
    wb = win_buf.shape[1]
    q_pos = past_len + jnp.arange(t)
    kc, vc = nsa_summaries(jnp.concatenate([past_cmp, cmp_rows], axis=1), p)
    slc = pad_rows(jnp.concatenate([past_slc, slc_rows], axis=1), SEL_BLOCK)
    win = jnp.concatenate([win_buf, win_rows], axis=1)
    w_pos = past_len - wb + jnp.arange(wb + t)
    return nsa_attend(q, g, q_pos, kc, vc, slc, win, w_pos)


def mla_queries(c_dq, q_norm_g, w_uq, w_uk, pos):
    b, t, _ = c_dq.shape
    q = (rmsnorm(c_dq, q_norm_g) @ w_uq).reshape(b, t, C_HEADS, C_NOPE + C_ROPE)
    q_lat = jnp.einsum('bthn,rhn->bthr', q[..., :C_NOPE], w_uk)
    return jnp.concatenate([q_lat, rope(q[..., C_NOPE:], pos)], axis=-1)


def mla_latents(c_dkv, kv_norm_g, pos):
    c = rmsnorm(c_dkv[..., :C_KV_LORA], kv_norm_g)
    return jnp.concatenate([c, rope(c_dkv[..., C_KV_LORA:], pos)], axis=-1)


def mla_attend(q_cat, q_pos, ckr, k_pos):
    s = jnp.einsum('bthr,bsr->bths', q_cat, ckr) * (C_NOPE + C_ROPE) ** -0.5
    mask = (k_pos[None, :] <= q_pos[:, None])[None, :, None, :]
    pr = masked_softmax(s, mask).astype(ckr.dtype)
    return jnp.einsum('bths,bsr->bthr', pr, ckr[..., :C_KV_LORA])


def mla_prompt(q_cat, ckr):
    t = q_cat.shape[1]
    qb = min(Q_BLOCK, t)
    k_pos = jnp.arange(t)

    def block(args):
        q_b, i = args
        return mla_attend(q_b, i * qb + jnp.arange(qb), ckr, k_pos)

    return from_blocks(lax.map(block, (to_blocks(q_cat, qb), jnp.arange(t // qb))))


def sublayer(x, p, pos, nsa_fn, mla_fn):
    b, t, _ = x.shape
    h = rmsnorm(x, p['pre_g'])
    (a_uv, a_z, b_q, b_kc, b_vc, b_ks, b_vs, b_kw, b_vw, b_g, b_z,
     c_dq, c_dkv, c_z, gates) = jnp.split(h @ p['w_in'], SPLIT_AT, axis=-1)
    u, v = jnp.split(jax.nn.gelu(a_uv), 2, axis=-1)
    y_a, v_rows = chunk_mlp(u, v, a_z, p['a_ln_g'], p['a_ln_b'], p['a_ws'], p['a_bs'])
    cmp_rows = jnp.concatenate([b_kc, b_vc], axis=-1)
    slc_rows = jnp.concatenate([b_ks, b_vs], axis=-1)
    win_rows = jnp.concatenate([b_kw, b_vw], axis=-1)
    o_b = nsa_fn(b_q.reshape(b, t, B_HEADS, B_DK), b_g.reshape(b, t, B_HEADS, 3), cmp_rows, slc_rows, win_rows)
    y_b = o_b.reshape(b, t, B_WIDTH) * jax.nn.silu(b_z)
    q_cat = mla_queries(c_dq, p['c_q_norm_g'], p['c_w_uq'], p['c_w_uk'], pos)
    ckr = mla_latents(c_dkv, p['c_kv_norm_g'], pos)
    o_lat = mla_fn(q_cat, ckr)
    y_c = jnp.einsum('bthr,rhv->bthv', o_lat, p['c_w_uv']).reshape(b, t, C_WIDTH) * jax.nn.silu(c_z)
    g_a, g_b, g_c = jnp.split(jax.nn.sigmoid(gates), 3, axis=-1)
    merged = g_a * (y_a @ p['w_proj_a']) + g_b * (y_b @ p['w_proj_b']) + g_c * (y_c @ p['w_proj_c'])
    x = x + rmsnorm(merged @ p['w_out'], p['post_g'])
    return x, v_rows, cmp_rows, slc_rows, win_rows, ckr


def setup_inputs(seed: int = 0) -> dict:
    key = jax.random.key(seed)
    keys = iter(jax.random.split(key, 40))

    def nrm(shape, scale):
        return scale * jax.random.normal(next(keys), shape, jnp.float32)

    def gain(shape):
        return 1.0 + nrm(shape, 0.02)

    n_pages = PAST_LEN // PAGE_SIZE
    n_used = DEC_BATCH * n_pages
    n_pool = n_used + max(1, n_used // 4)
    wb = min(WINDOW, PAST_LEN)
    row_b = B_DK + B_DV
    row_c = C_KV_LORA + C_ROPE
    x_prompt = nrm((BATCH, SEQ, D_MODEL), 1.0)
    x_sample = nrm((DEC_BATCH, DEC_SEQ, D_MODEL), 1.0)
    cache_mla = nrm((n_pool, DEPTH, PAGE_SIZE, row_c), 1.0)
    cache_nsa_cmp = nrm((n_pool, DEPTH, PAGE_SIZE, row_b), 1.0)
    cache_nsa_slc = nrm((n_pool, DEPTH, PAGE_SIZE, row_b), 1.0)
    state_nsa_win = nrm((DEC_BATCH, DEPTH, wb, row_b), 1.0)
    page_table = jax.random.permutation(next(keys), n_pool)[:n_used].reshape(DEC_BATCH, n_pages).astype(jnp.int32)
    return {
        'x_prompt': x_prompt,
        'x_sample': x_sample,
        'cache_mla': cache_mla,
        'cache_nsa_cmp': cache_nsa_cmp,
        'cache_nsa_slc': cache_nsa_slc,
        'state_nsa_win': state_nsa_win,
        'page_table': page_table,
        'pre_norm_g': gain((DEPTH, D_MODEL)),
        'w_in': nrm((DEPTH, D_MODEL, N_IN), D_MODEL ** -0.5),
        'a_ln_g': gain((DEPTH, A_WIDTH)),
        'a_ln_b': nrm((DEPTH, A_WIDTH), 0.02),
        'a_ws': nrm((DEPTH, A_GROUPS, A_CHUNK, A_CHUNK), A_CHUNK ** -0.5),
        'a_bs': gain((DEPTH, A_GROUPS, A_CHUNK)),
        'b_cmp_pe_k': nrm((DEPTH, CMP_LEN, B_DK), 0.02),
        'b_cmp_w1_k': nrm((DEPTH, CMP_LEN, B_DK, B_DK), (CMP_LEN * B_DK) ** -0.5),
        'b_cmp_w2_k': nrm((DEPTH, B_DK, B_DK), B_DK ** -0.5),
        'b_cmp_pe_v': nrm((DEPTH, CMP_LEN, B_DV), 0.02),
        'b_cmp_w1_v': nrm((DEPTH, CMP_LEN, B_DV, B_DV), (CMP_LEN * B_DV) ** -0.5),
        'b_cmp_w2_v': nrm((DEPTH, B_DV, B_DV), B_DV ** -0.5),
        'c_q_norm_g': gain((DEPTH, C_Q_LORA)),
        'c_kv_norm_g': gain((DEPTH, C_KV_LORA)),
        'c_w_uq': nrm((DEPTH, C_Q_LORA, C_HEADS * (C_NOPE + C_ROPE)), C_Q_LORA ** -0.5),
        'c_w_uk': nrm((DEPTH, C_KV_LORA, C_HEADS, C_NOPE), C_KV_LORA ** -0.5),
        'c_w_uv': nrm((DEPTH, C_KV_LORA, C_HEADS, C_VDIM), C_KV_LORA ** -0.5),
        'w_proj_a': nrm((DEPTH, A_WIDTH, D_MODEL), A_WIDTH ** -0.5),
        'w_proj_b': nrm((DEPTH, B_WIDTH, D_MODEL), B_WIDTH ** -0.5),
        'w_proj_c': nrm((DEPTH, C_WIDTH, D_MODEL), C_WIDTH ** -0.5),
        'w_out': nrm((DEPTH, D_MODEL, D_MODEL), D_MODEL ** -0.5),
        'post_norm_g': gain((DEPTH, D_MODEL)),
    }


def reference(x_prompt, x_sample, cache_mla, cache_nsa_cmp, cache_nsa_slc, state_nsa_win, page_table,
              pre_norm_g, w_in, a_ln_g, a_ln_b, a_ws, a_bs,
              b_cmp_pe_k, b_cmp_w1_k, b_cmp_w2_k, b_cmp_pe_v, b_cmp_w1_v, b_cmp_w2_v,
              c_q_norm_g, c_kv_norm_g, c_w_uq, c_w_uk, c_w_uv,
              w_proj_a, w_proj_b, w_proj_c, w_out, post_norm_g):
    past_len = page_table.shape[1] * PAGE_SIZE
    t_p = x_prompt.shape[1]
    t_s = x_sample.shape[1]
    pos_p = jnp.arange(t_p)
    pos_s = past_len + jnp.arange(t_s)
    k_pos_s = jnp.arange(past_len + t_s)
    xp, xs = x_prompt, x_sample
    mla_p, mla_s, cmp_p, cmp_s, slc_p, slc_s, win_p, win_s, av_s = [], [], [], [], [], [], [], [], []
    for l in range(DEPTH):
        p = {
            'pre_g': pre_norm_g[l], 'w_in': w_in[l],
            'a_ln_g': a_ln_g[l], 'a_ln_b': a_ln_b[l], 'a_ws': a_ws[l], 'a_bs': a_bs[l],
            'pe_k': b_cmp_pe_k[l], 'w1_k': b_cmp_w1_k[l], 'w2_k': b_cmp_w2_k[l],
            'pe_v': b_cmp_pe_v[l], 'w1_v': b_cmp_w1_v[l], 'w2_v': b_cmp_w2_v[l],
            'c_q_norm_g': c_q_norm_g[l], 'c_kv_norm_g': c_kv_norm_g[l],
            'c_w_uq': c_w_uq[l], 'c_w_uk': c_w_uk[l], 'c_w_uv': c_w_uv[l],
            'w_proj_a': w_proj_a[l], 'w_proj_b': w_proj_b[l], 'w_proj_c': w_proj_c[l],
            'w_out': w_out[l], 'post_g': post_norm_g[l],
        }
        xp, _, cr, sr, wr, ckr = sublayer(
            xp, p, pos_p,
            lambda q, g, c, s, w: nsa_prompt(q, g, c, s, w, p),
            mla_prompt)
        mla_p.append(ckr)
        cmp_p.append(cr)
        slc_p.append(sr)
        win_p.append(wr[:, -min(WINDOW, t_p):])
        past_mla = gather_pages(cache_mla, page_table, l)
        past_cmp = gather_pages(cache_nsa_cmp, page_table, l)
        past_slc = gather_pages(cache_nsa_slc, page_table, l)
        buf = state_nsa_win[:, l]
        xs, vr, cr, sr, wr, ckr = sublayer(
            xs, p, pos_s,
            lambda q, g, c, s, w: nsa_sample(q, g, c, s, w, p, past_cmp, past_slc, buf, past_len),
            lambda qc, kr: mla_attend(qc, pos_s, jnp.concatenate([past_mla, kr], axis=1), k_pos_s))
        mla_s.append(ckr)
        cmp_s.append(cr)
        slc_s.append(sr)
        win_s.append(jnp.concatenate([buf, wr], axis=1)[:, -buf.shape[1]:])
        av_s.append(vr)
    return (xp, xs,
            jnp.stack(mla_p, axis=1), jnp.stack(mla_s, axis=1),
            jnp.stack(cmp_p, axis=1), jnp.stack(cmp_s, axis=1),
            jnp.stack(slc_p, axis=1), jnp.stack(slc_s, axis=1),
            jnp.stack(win_p, axis=1), jnp.stack(win_s, axis=1),
            jnp.stack(av_s, axis=1))
```

```python
import functools

import numpy as np
import jax
import jax.numpy as jnp
from jax import lax
from jax.experimental import pallas as pl
from jax.experimental.pallas import tpu as pltpu

F32 = jnp.float32
BF16 = jnp.bfloat16

D_MODEL = 2048
DEPTH = 4
PAGE = 128
A_WIDTH = 1024
A_GROUPS = 8
A_CHUNK = 128
B_HEADS = 16
B_DK = 128
B_DV = 64
CMP_LEN = 32
CMP_STRIDE = 16
SEL_BLOCK = 64
N_SELECT = 16
WINDOW = 512
FORCE_BONUS = 1e4
C_HEADS = 8
C_Q_LORA = 512
C_KV_LORA = 128
C_NOPE = 128
C_ROPE = 64
ROPE_THETA = 10000.0
Q_BLOCK = 128
NORM_EPS = 1e-6
ROW_B = B_DK + B_DV
ROW_C = C_KV_LORA + C_ROPE
SLOT = 256
N_HALF = CMP_LEN // CMP_STRIDE

IN_WIDTHS = (2 * A_WIDTH, A_WIDTH, B_HEADS * B_DK, B_DK, B_DV, B_DK, B_DV, B_DK, B_DV, 3 * B_HEADS,
             B_HEADS * B_DV, C_Q_LORA, C_KV_LORA + C_ROPE, C_HEADS * 128, 3 * D_MODEL)
SPLIT_AT = tuple(int(s) for s in np.cumsum(IN_WIDTHS)[:-1])

OFF_GATES, OFF_AUV, OFF_BQ, OFF_AZ, OFF_BM, OFF_BZ, OFF_CM, OFF_CZ = (
    0, 6144, 8192, 10240, 11264, 12288, 13312, 14336)
PW = 15360
BM_CMP, BM_SLC, BM_WIN, BM_G = 0, 256, 512, 768
CM_DQ, CM_KV, CM_KR = 0, 512, 640

SCALE_B = B_DK ** -0.5
SCALE_C = (C_NOPE + C_ROPE) ** -0.5
NEG = -1e30
VMEM_LIMIT = 56 * 1024 * 1024


def _nt(a, b):
    return lax.dot_general(a, b, (((1,), (1,)), ((), ())), preferred_element_type=F32)


def _nn(a, b):
    return jnp.dot(a, b, preferred_element_type=F32)


def _split3(x):
    h = x.astype(BF16)
    r = x - h.astype(F32)
    m = r.astype(BF16)
    lo = (r - m.astype(F32)).astype(BF16)
    return h, m, lo


def _nn_exact_rhs01(x, w01):
    h, m, lo = _split3(x)
    return _nn(h, w01) + _nn(m, w01) + _nn(lo, w01)


def _nn_exact_lhs01(w01, x):
    h, m, lo = _split3(x)
    return _nn(w01, h) + _nn(w01, m) + _nn(w01, lo)


def _log2(n):
    assert n & (n - 1) == 0, n
    return n.bit_length() - 1


def _div(x, n):
    return jnp.right_shift(x, _log2(n))


def _mod(x, n):
    return jnp.bitwise_and(x, n - 1)


def _params(*sem):
    return pltpu.CompilerParams(dimension_semantics=sem, vmem_limit_bytes=VMEM_LIMIT)


def _rms(x, g):
    return x * lax.rsqrt(jnp.mean(x * x, axis=-1, keepdims=True) + NORM_EPS) * g


def _silu(x):
    return x * jax.nn.sigmoid(x)


def _inproj_kernel(x_ref, g_ref, w_ref, o_ref, h_ref):
    @pl.when(pl.program_id(1) == 0)
    def _():
        h_ref[...] = _rms(x_ref[...], g_ref[...]).astype(BF16)

    o_ref[...] = _nn(h_ref[...], w_ref[...])


def _inproj(x, g, w, tm=1024, tn=1024):
    m = x.shape[0]
    return pl.pallas_call(
        _inproj_kernel,
        grid=(m // tm, PW // tn),
        in_specs=[pl.BlockSpec((tm, D_MODEL), lambda i, j: (i, 0)),
                  pl.BlockSpec((1, D_MODEL), lambda i, j: (0, 0)),
                  pl.BlockSpec((D_MODEL, tn), lambda i, j: (0, j))],
        out_specs=pl.BlockSpec((tm, tn), lambda i, j: (i, j)),
        out_shape=jax.ShapeDtypeStruct((m, PW), F32),
        scratch_shapes=[pltpu.VMEM((tm, D_MODEL), BF16)],
        compiler_params=_params("arbitrary", "arbitrary"),
        name="in_proj",
    )(x, g, w)


def _mixa_kernel(auv_ref, az_ref, lng_ref, lnb_ref, ws_ref, bs_ref, y_ref, *vn_out, sample):
    auv = jax.nn.gelu(auv_ref[...])
    u = auv[:, :A_WIDTH]
    v = auv[:, A_WIDTH:]
    mu = jnp.mean(v, axis=-1, keepdims=True)
    var = jnp.mean(jnp.square(v - mu), axis=-1, keepdims=True)
    vn = (v - mu) * lax.rsqrt(var + NORM_EPS) * lng_ref[...] + lnb_ref[...]
    if vn_out:
        vn_out[0][...] = vn
    r = lax.broadcasted_iota(jnp.int32, (A_CHUNK, A_CHUNK), 0)
    c = lax.broadcasted_iota(jnp.int32, (A_CHUNK, A_CHUNK), 1)
    mask = c <= r
    if sample:
        mask = mask & (_div(r, 8) == _div(c, 8))
    vb = vn.astype(BF16)
    gw = A_WIDTH // A_GROUPS
    for g in range(A_GROUPS):
        w = jnp.where(mask, ws_ref[g], 0.0).astype(BF16)
        s = _nn(w, vb[:, g * gw:(g + 1) * gw]) + bs_ref[:, g * gw:(g + 1) * gw]
        y = u[:, g * gw:(g + 1) * gw] * s * _silu(az_ref[:, g * gw:(g + 1) * gw])
        y_ref[:, g * gw:(g + 1) * gw] = y.astype(BF16)


def _mixa(proj, row0, nrows, lng, lnb, ws, bs, sample):
    rb0 = row0 // A_CHUNK
    nb = nrows // A_CHUNK
    out_shape = [jax.ShapeDtypeStruct((nrows, A_WIDTH), BF16)]
    out_specs = [pl.BlockSpec((A_CHUNK, A_WIDTH), lambda i: (i, 0))]
    if sample:
        out_shape.append(jax.ShapeDtypeStruct((nrows, A_WIDTH), F32))
        out_specs.append(pl.BlockSpec((A_CHUNK, A_WIDTH), lambda i: (i, 0)))
    return pl.pallas_call(
        functools.partial(_mixa_kernel, sample=sample),
        grid=(nb,),
        in_specs=[pl.BlockSpec((A_CHUNK, 2 * A_WIDTH), lambda i: (rb0 + i, OFF_AUV // (2 * A_WIDTH))),
                  pl.BlockSpec((A_CHUNK, A_WIDTH), lambda i: (rb0 + i, OFF_AZ // A_WIDTH)),
                  pl.BlockSpec((1, A_WIDTH), lambda i: (0, 0)),
                  pl.BlockSpec((1, A_WIDTH), lambda i: (0, 0)),
                  pl.BlockSpec((A_GROUPS, A_CHUNK, A_CHUNK), lambda i: (0, 0, 0)),
                  pl.BlockSpec((A_CHUNK, A_WIDTH), lambda i: (0, 0))],
        out_specs=out_specs,
        out_shape=out_shape,
        compiler_params=_params("arbitrary"),
        name="mix_a_sample" if sample else "mix_a_prompt",
    )(proj, proj, lng, lnb, ws, bs)


def _mlaprep_kernel(cm_ref, tab_ref, qg_ref, kvg_ref, wuq_ref, wukt_ref, qcat_ref, ckr_ref):
    tab = tab_ref[...]
    h = _rms(cm_ref[:, CM_DQ:CM_DQ + C_Q_LORA], qg_ref[...]).astype(BF16)
    q = _nn(h, wuq_ref[...])
    lo = lax.broadcasted_iota(jnp.int32, tab.shape, 1) < C_ROPE
    for hh in range(C_HEADS):
        qn = q[:, hh * C_NOPE:(hh + 1) * C_NOPE].astype(BF16)
        qcat_ref[hh, :, 0:C_KV_LORA] = _nn(qn, wukt_ref[hh])
        pr = q[:, C_HEADS * C_NOPE + hh * 128:C_HEADS * C_NOPE + (hh + 1) * 128] * tab
        ro = pr + pltpu.roll(pr, 64, 1)
        qcat_ref[hh, :, C_KV_LORA:SLOT] = jnp.where(lo, ro, 0.0)
    ckr_ref[:, 0:C_KV_LORA] = _rms(cm_ref[:, CM_KV:CM_KV + C_KV_LORA], kvg_ref[...])
    prk = cm_ref[:, CM_KR:CM_KR + 128] * tab
    rok = prk + pltpu.roll(prk, 64, 1)
    ckr_ref[:, C_KV_LORA:ROW_C] = rok[:, 0:C_ROPE]


def _mlaprep(proj, tab, qg, kvg, wuq, wukt, tm=256):
    m = proj.shape[0]
    return pl.pallas_call(
        _mlaprep_kernel,
        grid=(m // tm,),
        in_specs=[pl.BlockSpec((tm, 1024), lambda i: (i, OFF_CM // 1024)),
                  pl.BlockSpec((tm, 128), lambda i: (i, 0)),
                  pl.BlockSpec((1, C_Q_LORA), lambda i: (0, 0)),
                  pl.BlockSpec((1, C_KV_LORA), lambda i: (0, 0)),
                  pl.BlockSpec((C_Q_LORA, 2048), lambda i: (0, 0)),
                  pl.BlockSpec((C_HEADS, C_NOPE, C_KV_LORA), lambda i: (0, 0, 0))],
        out_specs=[pl.BlockSpec((C_HEADS, tm, SLOT), lambda i: (0, i, 0)),
                   pl.BlockSpec((tm, ROW_C), lambda i: (i, 0))],
        out_shape=[jax.ShapeDtypeStruct((C_HEADS, m, SLOT), F32),
                   jax.ShapeDtypeStruct((m, ROW_C), F32)],
        compiler_params=_params("arbitrary"),
        name="mla_prep",
    )(proj, tab, qg, kvg, wuq, wukt)


def _mla_prompt_kernel(q_ref, k_ref, o_ref, kbf_ref, bias_ref):
    i = pl.program_id(1)
    t = k_ref.shape[0]

    @pl.when(i == 0)
    def _():
        kbf_ref[...] = jnp.zeros_like(kbf_ref)
        kbf_ref[:, 0:C_KV_LORA] = k_ref[:, 0:C_KV_LORA].astype(BF16)
        kbf_ref[:, C_KV_LORA:ROW_C] = k_ref[:, C_KV_LORA:ROW_C].astype(BF16)

    qpos = i * Q_BLOCK + lax.broadcasted_iota(jnp.int32, (Q_BLOCK, t), 0)
    kpos = lax.broadcasted_iota(jnp.int32, (Q_BLOCK, t), 1)
    bias_ref[...] = jnp.where(kpos <= qpos, 0.0, NEG)

    def head(h, carry):
        qh = q_ref[h].astype(BF16)
        s = _nt(qh, kbf_ref[...]) * SCALE_C + bias_ref[...]
        m = jnp.max(s, axis=-1, keepdims=True)
        e = jnp.exp(s - m)
        l = jnp.sum(e, axis=-1, keepdims=True)
        o_ref[h] = _nn(e.astype(BF16), kbf_ref[:, 0:C_KV_LORA]) / l
        return carry

    lax.fori_loop(0, C_HEADS, head, 0)


def _mla_prompt(qcat, ckr, nb, t):
    nq = t // Q_BLOCK
    return pl.pallas_call(
        _mla_prompt_kernel,
        grid=(nb, nq),
        in_specs=[pl.BlockSpec((C_HEADS, Q_BLOCK, SLOT), lambda b, i: (0, b * nq + i, 0)),
                  pl.BlockSpec((t, ROW_C), lambda b, i: (b, 0))],
        out_specs=pl.BlockSpec((C_HEADS, Q_BLOCK, C_KV_LORA), lambda b, i: (0, b * nq + i, 0)),
        out_shape=jax.ShapeDtypeStruct((C_HEADS, nb * t, C_KV_LORA), F32),
        scratch_shapes=[pltpu.VMEM((t, SLOT), BF16), pltpu.VMEM((Q_BLOCK, t), F32)],
        compiler_params=_params("arbitrary", "arbitrary"),
        name="mla_prompt",
    )(qcat, ckr)


def _pe_bias(pek_ref, pev_ref, wk_ref, wv_ref):
    pk = pek_ref[...].astype(BF16)
    pv = pev_ref[...].astype(BF16)
    bk = jnp.zeros((1, 128), F32)
    bv = jnp.zeros((1, 128), F32)
    for l in range(CMP_STRIDE):
        fk = _nn(pk, wk_ref[l])
        fv = _nn(pv, wv_ref[l])
        bk = bk + fk[l:l + 1, 0:128] + fk[CMP_STRIDE + l:CMP_STRIDE + l + 1, 128:256]
        bv = bv + fv[l:l + 1, 0:128] + fv[CMP_STRIDE + l:CMP_STRIDE + l + 1, 128:256]
    return bk, bv


def _compress(k_at, v_at, n_ch, n_out, bk, bv, wk_ref, wv_ref, w2k_ref, w2v_ref, acck_ref, accv_ref):
    acck = jnp.zeros((n_ch, 2 * B_DK), F32)
    accv = jnp.zeros((n_ch, 2 * 128), F32)
    for l in range(CMP_STRIDE):
        acck = acck + _nn(k_at(l).astype(BF16), wk_ref[l])
        accv = accv + _nn(v_at(l).astype(BF16), wv_ref[l])
    acck_ref[...] = acck
    accv_ref[...] = accv
    hk = acck_ref[0:n_out, 0:B_DK] + acck_ref[1:n_out + 1, B_DK:2 * B_DK] + bk
    hv = accv_ref[0:n_out, 0:128] + accv_ref[1:n_out + 1, 128:256] + bv
    kc = _nn(_silu(hk).astype(BF16), w2k_ref[...])
    vc = _nn(_silu(hv).astype(BF16), w2v_ref[...])
    return kc, vc


def _topk_mask(sc, nsel):
    nb, r = sc.shape
    idx = lax.broadcasted_iota(jnp.int32, (nb, r), 0).astype(F32)

    def body(_, carry):
        sc, sel = carry
        m = jnp.max(sc, axis=0, keepdims=True)
        ism = (sc == m) & (m > -jnp.inf)
        first = jnp.min(jnp.where(ism, idx, float(nb)), axis=0, keepdims=True)
        pick = idx == first
        return jnp.where(pick, -jnp.inf, sc), jnp.where(pick, 1.0, sel)

    _, sel = lax.fori_loop(0, nsel, body, (sc, jnp.zeros((nb, r), F32)))
    return sel


def _block_scores(imp, qpos):
    jj = lax.broadcasted_iota(jnp.int32, imp.shape, 1)
    cur = _div(qpos, SEL_BLOCK)
    valid = jj * SEL_BLOCK <= qpos
    forced = (jj == 0) | (jj == cur) | (jj == cur - 1)
    return jnp.where(valid, imp + jnp.where(forced, FORCE_BONUS, 0.0), -jnp.inf)


def _cmp_prompt_kernel(bm_ref, pek_ref, pev_ref, wk_ref, wv_ref, w2k_ref, w2v_ref, kc_ref, vc_ref,
                       ks_ref, vs_ref, acck_ref, accv_ref, *, t, n_ch, n_out):
    ks_ref[...] = jnp.zeros_like(ks_ref)
    vs_ref[...] = jnp.zeros_like(vs_ref)
    ks_ref[0:t, :] = bm_ref[:, BM_CMP:BM_CMP + B_DK]
    vs_ref[0:t, :] = bm_ref[:, BM_CMP + B_DK:BM_CMP + SLOT]
    bk, bv = _pe_bias(pek_ref, pev_ref, wk_ref, wv_ref)
    kc, vc = _compress(lambda l: ks_ref[pl.ds(l, n_ch, stride=CMP_STRIDE), :],
                       lambda l: vs_ref[pl.ds(l, n_ch, stride=CMP_STRIDE), :],
                       n_ch, n_out, bk, bv, wk_ref, wv_ref, w2k_ref, w2v_ref, acck_ref, accv_ref)
    kc_ref[...] = kc
    vc_ref[...] = vc


def _cmp_prompt(proj, nb, t, cw):
    n_out = t // CMP_STRIDE
    n_ch = n_out + 8
    rows = n_ch * CMP_STRIDE
    return pl.pallas_call(
        functools.partial(_cmp_prompt_kernel, t=t, n_ch=n_ch, n_out=n_out),
        grid=(nb,),
        in_specs=[pl.BlockSpec((t, 1024), lambda b: (b, OFF_BM // 1024)),
                  pl.BlockSpec((CMP_LEN, 128), lambda b: (0, 0)),
                  pl.BlockSpec((CMP_LEN, 128), lambda b: (0, 0)),
                  pl.BlockSpec((CMP_STRIDE, 128, 256), lambda b: (0, 0, 0)),
                  pl.BlockSpec((CMP_STRIDE, 128, 256), lambda b: (0, 0, 0)),
                  pl.BlockSpec((128, 128), lambda b: (0, 0)),
                  pl.BlockSpec((128, 128), lambda b: (0, 0))],
        out_specs=[pl.BlockSpec((None, n_out, 128), lambda b: (b, 0, 0)),
                   pl.BlockSpec((None, n_out, 128), lambda b: (b, 0, 0))],
        out_shape=[jax.ShapeDtypeStruct((nb, n_out, 128), F32),
                   jax.ShapeDtypeStruct((nb, n_out, 128), F32)],
        scratch_shapes=[pltpu.VMEM((rows, 128), F32), pltpu.VMEM((rows, 128), F32),
                        pltpu.VMEM((n_ch, 256), F32), pltpu.VMEM((n_ch, 256), F32)],
        compiler_params=_params("arbitrary"),
        name="cmp_prompt",
    )(proj, cw["pek"], cw["pev"], cw["wk"], cw["wv"], cw["w2k"], cw["w2v"])


def _nsa_prompt_kernel(q_ref, g_ref, bm_ref, kc_ref, vc_ref, cover_ref, e_ref, o_ref,
                       qh_ref, slk_ref, slv_ref, wk_ref, wv_ref, kcb_ref, vcb_ref,
                       ocmp_ref, gsp_ref, bias_s_ref, bias_w_ref, opair_ref, *, t):
    i = pl.program_id(1)
    n_pair = B_HEADS // 2
    wrows = WINDOW + Q_BLOCK
    lane = lax.broadcasted_iota(jnp.int32, (Q_BLOCK, 128), 1)
    low = lane < B_DV

    @pl.when(i == 0)
    def _():
        slk_ref[...] = bm_ref[:, BM_SLC:BM_SLC + B_DK].astype(BF16)
        v = bm_ref[:, BM_SLC + B_DK:BM_SLC + SLOT]
        slv_ref[0] = v.astype(BF16)
        slv_ref[1] = pltpu.roll(v, 64, 1).astype(BF16)
        wk_ref[...] = bm_ref[:, BM_WIN:BM_WIN + B_DK].astype(BF16)
        v = bm_ref[:, BM_WIN + B_DK:BM_WIN + SLOT]
        wv_ref[0] = v.astype(BF16)
        wv_ref[1] = pltpu.roll(v, 64, 1).astype(BF16)
        kcb_ref[...] = kc_ref[...].astype(BF16)
        vcb_ref[0] = vc_ref[...].astype(BF16)
        vcb_ref[1] = pltpu.roll(vc_ref[...], 64, 1).astype(BF16)

    for h in range(B_HEADS):
        qh_ref[h] = q_ref[:, h * B_DK:(h + 1) * B_DK].astype(BF16)

    gs = jax.nn.sigmoid(g_ref[:, BM_G:BM_G + 128])
    for p in range(n_pair):
        for j in range(3):
            ce = 3 * (2 * p) + j
            co = 3 * (2 * p + 1) + j
            gsp_ref[3 * p + j] = jnp.where(low, jnp.broadcast_to(gs[:, ce:ce + 1], (Q_BLOCK, 128)),
                                           jnp.broadcast_to(gs[:, co:co + 1], (Q_BLOCK, 128)))

    qpos1 = i * Q_BLOCK + lax.broadcasted_iota(jnp.int32, (Q_BLOCK, 1), 0)

    n_c = kcb_ref.shape[0]
    c_end = lax.broadcasted_iota(jnp.int32, (Q_BLOCK, n_c), 1) * CMP_STRIDE + (CMP_LEN - 1)
    c_ok = c_end <= qpos1

    def cmp_pair(p, psum):
        acc = jnp.zeros((Q_BLOCK, 128), F32)
        for hh in range(2):
            s = jnp.where(c_ok, _nt(qh_ref[2 * p + hh], kcb_ref[...]) * SCALE_B, NEG)
            m = jnp.max(s, axis=-1, keepdims=True)
            e = jnp.where(c_ok, jnp.exp(s - m), 0.0)
            pr = e / jnp.maximum(jnp.sum(e, axis=-1, keepdims=True), 1e-30)
            psum = psum + pr
            acc = acc + _nn(pr.astype(BF16), vcb_ref[hh])
        ocmp_ref[p] = acc
        return psum

    psum = lax.fori_loop(0, n_pair, cmp_pair, jnp.zeros((Q_BLOCK, n_c), F32))

    imp = _nn_exact_rhs01(psum, cover_ref[...])
    score = _block_scores(imp, qpos1)
    n_s = t // SEL_BLOCK
    sel_t = _topk_mask(score.T[0:n_s, :], min(N_SELECT, n_s))
    sel = jnp.concatenate([sel_t, jnp.zeros((128 - n_s, Q_BLOCK), F32)], axis=0).T
    in_sel = _nn(sel.astype(BF16), e_ref[...])
    kpos = lax.broadcasted_iota(jnp.int32, (Q_BLOCK, t), 1)
    bias_s_ref[...] = jnp.where((in_sel > 0.5) & (kpos <= qpos1), 0.0, NEG)

    ws = pl.multiple_of(jnp.maximum(i * Q_BLOCK - WINDOW, 0), Q_BLOCK)
    wpos = ws + lax.broadcasted_iota(jnp.int32, (Q_BLOCK, wrows), 1)
    bias_w_ref[...] = jnp.where((wpos <= qpos1) & (wpos >= qpos1 - WINDOW), 0.0, NEG)

    def attend(qh, k, v, bias):
        s = _nt(qh, k) * SCALE_B + bias
        m = jnp.max(s, axis=-1, keepdims=True)
        e = jnp.exp(s - m)
        l = jnp.sum(e, axis=-1, keepdims=True)
        return _nn(e.astype(BF16), v) / l

    def pair(p, carry):
        osel = jnp.zeros((Q_BLOCK, 128), F32)
        owin = jnp.zeros((Q_BLOCK, 128), F32)
        for hh in range(2):
            qh = qh_ref[2 * p + hh]
            osel = osel + attend(qh, slk_ref[...], slv_ref[hh], bias_s_ref[...])
            owin = owin + attend(qh, wk_ref[pl.ds(ws, wrows), :], wv_ref[hh, pl.ds(ws, wrows), :], bias_w_ref[...])
        opair_ref[p] = gsp_ref[3 * p] * ocmp_ref[p] + gsp_ref[3 * p + 1] * osel + gsp_ref[3 * p + 2] * owin
        return carry

    lax.fori_loop(0, n_pair, pair, 0)
    for p in range(n_pair):
        o_ref[:, p * 128:(p + 1) * 128] = opair_ref[p]


def _nsa_prompt(proj, kc, vc, cover, emat, nb, t):
    nq = t // Q_BLOCK
    n_c = kc.shape[1]
    n_pair = B_HEADS // 2
    return pl.pallas_call(
        functools.partial(_nsa_prompt_kernel, t=t),
        grid=(nb, nq),
        in_specs=[pl.BlockSpec((Q_BLOCK, B_HEADS * B_DK), lambda b, i: (b * nq + i, OFF_BQ // (B_HEADS * B_DK))),
                  pl.BlockSpec((Q_BLOCK, 1024), lambda b, i: (b * nq + i, OFF_BM // 1024)),
                  pl.BlockSpec((t, 1024), lambda b, i: (b, OFF_BM // 1024)),
                  pl.BlockSpec((None, n_c, 128), lambda b, i: (b, 0, 0)),
                  pl.BlockSpec((None, n_c, 128), lambda b, i: (b, 0, 0)),
                  pl.BlockSpec((n_c, 128), lambda b, i: (0, 0)),
                  pl.BlockSpec((128, t), lambda b, i: (0, 0))],
        out_specs=pl.BlockSpec((Q_BLOCK, B_HEADS * B_DV), lambda b, i: (b * nq + i, 0)),
        out_shape=jax.ShapeDtypeStruct((nb * t, B_HEADS * B_DV), F32),
        scratch_shapes=[pltpu.VMEM((B_HEADS, Q_BLOCK, B_DK), BF16),
                        pltpu.VMEM((t, B_DK), BF16), pltpu.VMEM((2, t, 128), BF16),
                        pltpu.VMEM((t, B_DK), BF16), pltpu.VMEM((2, t, 128), BF16),
                        pltpu.VMEM((n_c, B_DK), BF16), pltpu.VMEM((2, n_c, 128), BF16),
                        pltpu.VMEM((n_pair, Q_BLOCK, 128), F32),
                        pltpu.VMEM((3 * n_pair, Q_BLOCK, 128), F32),
                        pltpu.VMEM((Q_BLOCK, t), F32),
                        pltpu.VMEM((Q_BLOCK, WINDOW + Q_BLOCK), F32),
                        pltpu.VMEM((n_pair, Q_BLOCK, 128), F32)],
        compiler_params=_params("arbitrary", "arbitrary"),
        name="nsa_prompt",
    )(proj, proj, proj, kc, vc, cover, emat)


def _page_copies(pt_ref, cache_ref, layer, bb, slot, n_pages, dsts, sem):
    def at(j):
        pg = pt_ref[bb, j]
        return [pltpu.make_async_copy(
            cache_ref.at[pg, layer] if w == cache_ref.shape[-1] else cache_ref.at[pg, layer, :, pl.ds(c0, w)],
            buf.at[slot, pl.ds(j * PAGE, PAGE)] if w == buf.shape[-1] else buf.at[slot, pl.ds(j * PAGE, PAGE), pl.ds(0, w)],
            sem.at[slot]) for buf, c0, w in dsts]
    return at


def _gather_step(pt_ref, cache_ref, layer, n_pages, dsts, sem, init):
    b = pl.program_id(0)
    nb = pl.num_programs(0)
    slot = lax.rem(b, 2)

    def start(bb, sl):
        at = _page_copies(pt_ref, cache_ref, layer, bb, sl, n_pages, dsts, sem)

        def body(j, c):
            for cp in at(j):
                cp.start()
            return c
        lax.fori_loop(0, n_pages, body, 0)

    @pl.when(b == 0)
    def _():
        init()
        start(0, 0)

    @pl.when(b + 1 < nb)
    def _():
        start(b + 1, 1 - slot)

    at = _page_copies(pt_ref, cache_ref, layer, b, slot, n_pages, dsts, sem)

    def wbody(j, c):
        for cp in at(j):
            cp.wait()
        return c
    lax.fori_loop(0, n_pages, wbody, 0)
    return slot


def _online_chunk(s, v, carry):
    m, l, acc = carry
    m_new = jnp.maximum(m, jnp.max(s, axis=-1, keepdims=True))
    a = jnp.exp(m - m_new)
    e = jnp.exp(s - m_new)
    return m_new, a * l + jnp.sum(e, axis=-1, keepdims=True), a * acc + _nn(e.astype(BF16), v)


def _mla_sample_kernel(pt_ref, q_ref, new_ref, cache_ref, o_ref, kbuf_ref, tail_ref, sem, *, layer, n_pages, ts, chunk):
    slot = _gather_step(pt_ref, cache_ref, layer, n_pages, [(kbuf_ref, 0, ROW_C)], sem, lambda: None)
    rows = C_HEADS * ts
    qb = q_ref[...].reshape(rows, SLOT).astype(BF16)
    qc = qb[:, 0:C_KV_LORA]
    qr = qb[:, C_KV_LORA:ROW_C]
    tail_ref[...] = jnp.zeros_like(tail_ref)
    tail_ref[0:ts, :] = new_ref[...]

    def scores(k):
        return (_nt(qc, k[:, 0:C_KV_LORA]) + _nt(qr, k[:, C_KV_LORA:ROW_C])) * SCALE_C

    def body(c, carry):
        k = kbuf_ref[slot, pl.ds(pl.multiple_of(c * chunk, chunk), chunk), :].astype(BF16)
        return _online_chunk(scores(k), k[:, 0:C_KV_LORA], carry)

    carry = (jnp.full((rows, 1), NEG, F32), jnp.zeros((rows, 1), F32), jnp.zeros((rows, C_KV_LORA), F32))
    carry = lax.fori_loop(0, n_pages * PAGE // chunk, body, carry)
    tq = _mod(lax.broadcasted_iota(jnp.int32, (rows, PAGE), 0), ts)
    tk = lax.broadcasted_iota(jnp.int32, (rows, PAGE), 1)
    kt = tail_ref[...].astype(BF16)
    _, l, acc = _online_chunk(scores(kt) + jnp.where(tk <= tq, 0.0, NEG), kt[:, 0:C_KV_LORA], carry)
    o = acc / l
    for h in range(C_HEADS):
        o_ref[h] = o[h * ts:(h + 1) * ts, :]


def _mla_sample(page_table, qcat, ckr, cache, layer, row0, nb, ts, chunk=2048):
    n_pages = page_table.shape[1]
    rb0 = row0 // ts
    grid_spec = pltpu.PrefetchScalarGridSpec(
        num_scalar_prefetch=1,
        grid=(nb,),
        in_specs=[pl.BlockSpec((C_HEADS, ts, SLOT), lambda b, pt: (0, rb0 + b, 0)),
                  pl.BlockSpec((ts, ROW_C), lambda b, pt: (rb0 + b, 0)),
                  pl.BlockSpec(memory_space=pl.ANY)],
        out_specs=pl.BlockSpec((C_HEADS, ts, C_KV_LORA), lambda b, pt: (0, b, 0)),
        scratch_shapes=[pltpu.VMEM((2, n_pages * PAGE, ROW_C), F32), pltpu.VMEM((PAGE, ROW_C), F32),
                        pltpu.SemaphoreType.DMA((2,))],
    )
    return pl.pallas_call(
        functools.partial(_mla_sample_kernel, layer=layer, n_pages=n_pages, ts=ts, chunk=chunk),
        grid_spec=grid_spec,
        out_shape=jax.ShapeDtypeStruct((C_HEADS, nb * ts, C_KV_LORA), F32),
        compiler_params=_params("arbitrary"),
        name="mla_sample",
    )(page_table, qcat, ckr, cache)


def _cmp_sample_kernel(pt_ref, q_ref, bm_ref, cache_ref, pek_ref, pev_ref, wk_ref, wv_ref, w2k_ref, w2v_ref,
                       cover_ref, gsum_ref, ocmp_ref, sel_ref, ck_ref, vraw_ref, cv_ref, acck_ref, accv_ref, pb_ref, sem,
                       *, layer, n_pages, ts, n_ch, n_out, past):
    def init():
        ck_ref[...] = jnp.zeros_like(ck_ref)
        cv_ref[...] = jnp.zeros_like(cv_ref)
        bk, bv = _pe_bias(pek_ref, pev_ref, wk_ref, wv_ref)
        pb_ref[0:1, :] = bk
        pb_ref[1:2, :] = bv

    slot = _gather_step(pt_ref, cache_ref, layer, n_pages, [(ck_ref, 0, B_DK), (vraw_ref, B_DK, B_DV)], sem, init)
    rows = B_HEADS * ts
    cv_ref[0:past, 0:B_DV] = vraw_ref[slot]
    ck_ref[slot, past:past + ts, :] = bm_ref[:, BM_CMP:BM_CMP + B_DK]
    cv_ref[past:past + ts, :] = bm_ref[:, BM_CMP + B_DK:BM_CMP + SLOT]
    kc, vc = _compress(lambda l: ck_ref[slot, pl.ds(l, n_ch, stride=CMP_STRIDE), :],
                       lambda l: cv_ref[pl.ds(l, n_ch, stride=CMP_STRIDE), :],
                       n_ch, n_out, pb_ref[0:1, :], pb_ref[1:2, :], wk_ref, wv_ref, w2k_ref, w2v_ref,
                       acck_ref, accv_ref)
    qb = jnp.concatenate([q_ref[:, h * B_DK:(h + 1) * B_DK] for h in range(B_HEADS)], axis=0).astype(BF16)
    qpos = past + _mod(lax.broadcasted_iota(jnp.int32, (rows, 1), 0), ts)
    c_end = lax.broadcasted_iota(jnp.int32, (rows, n_out), 1) * CMP_STRIDE + (CMP_LEN - 1)
    c_ok = c_end <= qpos
    s = jnp.where(c_ok, _nt(qb, kc.astype(BF16)) * SCALE_B, NEG)
    m = jnp.max(s, axis=-1, keepdims=True)
    e = jnp.where(c_ok, jnp.exp(s - m), 0.0)
    pr = e / jnp.maximum(jnp.sum(e, axis=-1, keepdims=True), 1e-30)
    ocmp_ref[...] = _nn(pr.astype(BF16), vc.astype(BF16))
    imp = _nn_exact_lhs01(gsum_ref[...], _nn_exact_rhs01(pr, cover_ref[...]))
    score = _block_scores(imp, qpos)
    n_s = (past + ts + SEL_BLOCK - 1) // SEL_BLOCK
    n_sp = -(-n_s // 8) * 8
    sel_t = _topk_mask(score.T[0:n_sp, :], min(N_SELECT, n_s))
    sel = jnp.concatenate([sel_t, jnp.zeros((imp.shape[1] - n_sp, rows), F32)], axis=0).T
    sel_ref[...] = sel[0:ts, :]


def _cmp_sample(page_table, proj, cache, cw, cover, gsum, layer, row0, nb, ts):
    n_pages = page_table.shape[1]
    past = n_pages * PAGE
    rb0 = row0 // ts
    n_out = -(-(past + ts) // CMP_STRIDE) - N_HALF + 1
    n_ch = n_out + 8
    rows_buf = n_ch * CMP_STRIDE
    rows = B_HEADS * ts
    n_slots = cover.shape[1]
    const = lambda shape: pl.BlockSpec(shape, lambda b, pt: (0,) * len(shape))
    grid_spec = pltpu.PrefetchScalarGridSpec(
        num_scalar_prefetch=1,
        grid=(nb,),
        in_specs=[pl.BlockSpec((ts, B_HEADS * B_DK), lambda b, pt: (rb0 + b, OFF_BQ // (B_HEADS * B_DK))),
                  pl.BlockSpec((ts, 1024), lambda b, pt: (rb0 + b, OFF_BM // 1024)),
                  pl.BlockSpec(memory_space=pl.ANY),
                  const((CMP_LEN, 128)), const((CMP_LEN, 128)),
                  const((CMP_STRIDE, 128, 256)), const((CMP_STRIDE, 128, 256)),
                  const((128, 128)), const((128, 128)),
                  const((n_out, n_slots)), const((rows, rows))],
        out_specs=[pl.BlockSpec((None, rows, 128), lambda b, pt: (b, 0, 0)),
                   pl.BlockSpec((None, ts, n_slots), lambda b, pt: (b, 0, 0))],
        scratch_shapes=[pltpu.VMEM((2, rows_buf, 128), F32), pltpu.VMEM((2, past, B_DV), F32),
                        pltpu.VMEM((rows_buf, 128), F32),
                        pltpu.VMEM((n_ch, 256), F32), pltpu.VMEM((n_ch, 256), F32),
                        pltpu.VMEM((8, 128), F32), pltpu.SemaphoreType.DMA((2,))],
    )
    return pl.pallas_call(
        functools.partial(_cmp_sample_kernel, layer=layer, n_pages=n_pages, ts=ts, n_ch=n_ch, n_out=n_out, past=past),
        grid_spec=grid_spec,
        out_shape=[jax.ShapeDtypeStruct((nb, rows, 128), F32),
                   jax.ShapeDtypeStruct((nb, ts, n_slots), F32)],
        compiler_params=_params("arbitrary"),
        name="cmp_sample",
    )(page_table, proj, proj, cache, cw["pek"], cw["pev"], cw["wk"], cw["wv"], cw["w2k"], cw["w2v"], cover, gsum)


def _slc_sample_kernel(pt_ref, q_ref, bm_ref, sel_ref, ocmp_ref, win_ref, e_ref, et_ref, cache_ref,
                       o_ref, nwin_ref, sbuf_ref, tail_ref, wscr_ref, opad_ref, sem, *, layer, n_pages, ts, chunk, past):
    slot = _gather_step(pt_ref, cache_ref, layer, n_pages, [(sbuf_ref, 0, ROW_B)], sem, lambda: None)
    rows = B_HEADS * ts
    wb = win_ref.shape[0]
    new_slc = bm_ref[:, BM_SLC:BM_SLC + SLOT]
    new_win = bm_ref[:, BM_WIN:BM_WIN + SLOT]
    qb = jnp.concatenate([q_ref[:, h * B_DK:(h + 1) * B_DK] for h in range(B_HEADS)], axis=0).astype(BF16)
    tq = _mod(lax.broadcasted_iota(jnp.int32, (rows, 1), 0), ts)
    sel = jnp.concatenate([sel_ref[...]] * B_HEADS, axis=0).astype(BF16)

    def body(c, carry):
        k = sbuf_ref[slot, pl.ds(pl.multiple_of(c * chunk, chunk), chunk), :].astype(BF16)
        in_sel = _nn(sel[:, 0:128], e_ref[c])
        s = _nt(qb, k[:, 0:B_DK]) * SCALE_B + jnp.where(in_sel > 0.5, 0.0, NEG)
        return _online_chunk(s, k[:, B_DK:ROW_B], carry)

    carry = (jnp.full((rows, 1), NEG, F32), jnp.zeros((rows, 1), F32), jnp.zeros((rows, B_DV), F32))
    carry = lax.fori_loop(0, past // chunk, body, carry)
    tail_ref[...] = jnp.zeros_like(tail_ref)
    tail_ref[0:ts, :] = new_slc
    kt = tail_ref[...].astype(BF16)
    tk = lax.broadcasted_iota(jnp.int32, (rows, PAGE), 1)
    in_sel = _nn(sel[:, 128:256], et_ref[...])
    bias = jnp.where((in_sel > 0.5) & (tk <= tq), 0.0, NEG)
    _, l, acc = _online_chunk(_nt(qb, kt[:, 0:B_DK]) * SCALE_B + bias, kt[:, B_DK:ROW_B], carry)
    osel = acc / l

    wscr_ref[...] = jnp.zeros_like(wscr_ref)
    wscr_ref[0:wb, 0:B_DK] = win_ref[:, 0:B_DK]
    wscr_ref[0:wb, B_DK:ROW_B] = win_ref[:, B_DK:ROW_B]
    wscr_ref[wb:wb + ts, :] = new_win
    kw = wscr_ref[...].astype(BF16)
    wpos = past - wb + lax.broadcasted_iota(jnp.int32, (rows, wscr_ref.shape[0]), 1)
    qpos = past + tq
    bias_w = jnp.where((wpos <= qpos) & (wpos >= qpos - WINDOW) & (wpos < past + ts), 0.0, NEG)
    s = _nt(qb, kw[:, 0:B_DK]) * SCALE_B + bias_w
    m = jnp.max(s, axis=-1, keepdims=True)
    e = jnp.exp(s - m)
    owin = _nn(e.astype(BF16), kw[:, B_DK:ROW_B]) / jnp.sum(e, axis=-1, keepdims=True)

    gs = jnp.concatenate([jax.nn.sigmoid(bm_ref[:, BM_G:BM_G + 128])] * B_HEADS, axis=0)
    col = lax.broadcasted_iota(jnp.int32, (rows, 128), 1)
    hd = _div(lax.broadcasted_iota(jnp.int32, (rows, 128), 0), ts)

    def gate(j):
        return jnp.sum(jnp.where(col == 3 * hd + j, gs, 0.0), axis=-1, keepdims=True)

    opad_ref[...] = jnp.zeros_like(opad_ref)
    opad_ref[:, 0:B_DV] = gate(0) * ocmp_ref[:, 0:B_DV] + gate(1) * osel + gate(2) * owin
    o = opad_ref[...]
    for p in range(B_HEADS // 2):
        lo = o[(2 * p) * ts:(2 * p + 1) * ts, :]
        hi = pltpu.roll(o[(2 * p + 1) * ts:(2 * p + 2) * ts, :], 64, 1)
        o_ref[:, p * 128:(p + 1) * 128] = lo + hi

    nwin_ref[0:wb - ts, :] = win_ref[ts:wb, :]
    nwin_ref[wb - ts:wb, :] = new_win[:, 0:ROW_B]


def _slc_sample(page_table, proj, sel, ocmp, state_win, emat, etail, cache, layer, row0, nb, ts, chunk=2048):
    n_pages = page_table.shape[1]
    past = n_pages * PAGE
    rb0 = row0 // ts
    rows = B_HEADS * ts
    wb = state_win.shape[2]
    n_slots = sel.shape[2]
    wrows = -(-(wb + ts) // 128) * 128
    grid_spec = pltpu.PrefetchScalarGridSpec(
        num_scalar_prefetch=1,
        grid=(nb,),
        in_specs=[pl.BlockSpec((ts, B_HEADS * B_DK), lambda b, pt: (rb0 + b, OFF_BQ // (B_HEADS * B_DK))),
                  pl.BlockSpec((ts, 1024), lambda b, pt: (rb0 + b, OFF_BM // 1024)),
                  pl.BlockSpec((None, ts, n_slots), lambda b, pt: (b, 0, 0)),
                  pl.BlockSpec((None, rows, 128), lambda b, pt: (b, 0, 0)),
                  pl.BlockSpec((None, None, wb, ROW_B), lambda b, pt: (b, layer, 0, 0)),
                  pl.BlockSpec(emat.shape, lambda b, pt: (0, 0, 0)),
                  pl.BlockSpec(etail.shape, lambda b, pt: (0, 0)),
                  pl.BlockSpec(memory_space=pl.ANY)],
        out_specs=[pl.BlockSpec((ts, B_HEADS * B_DV), lambda b, pt: (b, 0)),
                   pl.BlockSpec((None, wb, ROW_B), lambda b, pt: (b, 0, 0))],
        scratch_shapes=[pltpu.VMEM((2, past, ROW_B), F32), pltpu.VMEM((PAGE, SLOT), F32),
                        pltpu.VMEM((wrows, SLOT), F32), pltpu.VMEM((rows, 128), F32),
                        pltpu.SemaphoreType.DMA((2,))],
    )
    return pl.pallas_call(
        functools.partial(_slc_sample_kernel, layer=layer, n_pages=n_pages, ts=ts, chunk=chunk, past=past),
        grid_spec=grid_spec,
        out_shape=[jax.ShapeDtypeStruct((nb * ts, B_HEADS * B_DV), F32),
                   jax.ShapeDtypeStruct((nb, wb, ROW_B), F32)],
        compiler_params=_params("arbitrary"),
        name="slc_sample",
    )(page_table, proj, proj, sel, ocmp, state_win, emat, etail, cache)


def _merge_kernel(ya_ref, ob_ref, olat_ref, bz_ref, cz_ref, gates_ref, wuv_ref, wpa_ref, wpb_ref, wpc_ref, o_ref):
    yb = (ob_ref[...] * _silu(bz_ref[...])).astype(BF16)
    acc = jax.nn.sigmoid(gates_ref[:, 0:D_MODEL]) * _nn(ya_ref[...], wpa_ref[...])
    acc = acc + jax.nn.sigmoid(gates_ref[:, D_MODEL:2 * D_MODEL]) * _nn(yb, wpb_ref[...])
    yc = jnp.concatenate([_nn(olat_ref[h].astype(BF16), wuv_ref[h]) for h in range(C_HEADS)], axis=1)
    yc = (yc * _silu(cz_ref[...])).astype(BF16)
    acc = acc + jax.nn.sigmoid(gates_ref[:, 2 * D_MODEL:3 * D_MODEL]) * _nn(yc, wpc_ref[...])
    o_ref[...] = acc.astype(BF16)


def _merge(ya, ob, olat, proj, wuv, wpa, wpb, wpc, tm=256):
    m = ya.shape[0]
    const = lambda shape: pl.BlockSpec(shape, lambda i: (0,) * len(shape))
    return pl.pallas_call(
        _merge_kernel,
        grid=(m // tm,),
        in_specs=[pl.BlockSpec((tm, A_WIDTH), lambda i: (i, 0)),
                  pl.BlockSpec((tm, B_HEADS * B_DV), lambda i: (i, 0)),
                  pl.BlockSpec((C_HEADS, tm, C_KV_LORA), lambda i: (0, i, 0)),
                  pl.BlockSpec((tm, 1024), lambda i: (i, OFF_BZ // 1024)),
                  pl.BlockSpec((tm, 1024), lambda i: (i, OFF_CZ // 1024)),
                  pl.BlockSpec((tm, 3 * D_MODEL), lambda i: (i, OFF_GATES // (3 * D_MODEL))),
                  const((C_HEADS, C_KV_LORA, 128)), const((A_WIDTH, D_MODEL)),
                  const((B_HEADS * B_DV, D_MODEL)), const((C_HEADS * 128, D_MODEL))],
        out_specs=pl.BlockSpec((tm, D_MODEL), lambda i: (i, 0)),
        out_shape=jax.ShapeDtypeStruct((m, D_MODEL), BF16),
        compiler_params=_params("arbitrary"),
        name="merge",
    )(ya, ob, olat, proj, proj, proj, wuv, wpa, wpb, wpc)


def _outproj_kernel(mg_ref, x_ref, w_ref, g_ref, o_ref):
    out = _nn(mg_ref[...], w_ref[...])
    o_ref[...] = x_ref[...] + _rms(out, g_ref[...])


def _outproj(merged, x, w, g, tm=512):
    m = x.shape[0]
    return pl.pallas_call(
        _outproj_kernel,
        grid=(m // tm,),
        in_specs=[pl.BlockSpec((tm, D_MODEL), lambda i: (i, 0)),
                  pl.BlockSpec((tm, D_MODEL), lambda i: (i, 0)),
                  pl.BlockSpec((D_MODEL, D_MODEL), lambda i: (0, 0)),
                  pl.BlockSpec((1, D_MODEL), lambda i: (0, 0))],
        out_specs=pl.BlockSpec((tm, D_MODEL), lambda i: (i, 0)),
        out_shape=jax.ShapeDtypeStruct((m, D_MODEL), F32),
        compiler_params=_params("arbitrary"),
        name="out_proj",
    )(merged, x, w, g)


def _pack_w_in(w_in):
    (a_uv, a_z, b_q, b_kc, b_vc, b_ks, b_vs, b_kw, b_vw, b_g, b_z, c_dq, c_dkv, c_z, gates) = jnp.split(
        w_in, SPLIT_AT, axis=-1)
    z = lambda n: jnp.zeros(w_in.shape[:-1] + (n,), w_in.dtype)
    kr = c_dkv[..., C_KV_LORA:]
    kr_sw = jnp.concatenate([kr[..., C_ROPE // 2:], kr[..., :C_ROPE // 2]], axis=-1)
    bm = jnp.concatenate([b_kc, b_vc, z(64), b_ks, b_vs, z(64), b_kw, b_vw, z(64), b_g, z(1024 - 768 - 48)], axis=-1)
    cm = jnp.concatenate([c_dq, c_dkv[..., :C_KV_LORA], kr, kr_sw, z(1024 - 768)], axis=-1)
    return jnp.concatenate([gates, a_uv, b_q, a_z, bm, b_z, cm, c_z], axis=-1).astype(BF16)


def _pack_w_uq(w_uq):
    d = w_uq.shape[0]
    w = w_uq.reshape(d, C_Q_LORA, C_HEADS, C_NOPE + C_ROPE)
    nope = w[..., :C_NOPE].reshape(d, C_Q_LORA, C_HEADS * C_NOPE)
    r = w[..., C_NOPE:]
    r_sw = jnp.concatenate([r[..., C_ROPE // 2:], r[..., :C_ROPE // 2]], axis=-1)
    rope = jnp.concatenate([r, r_sw], axis=-1).reshape(d, C_Q_LORA, C_HEADS * 128)
    return jnp.concatenate([nope, rope], axis=-1).astype(BF16)


def _pack_cmp(pe_k, w1_k, w2_k, pe_v, w1_v, w2_v):
    d = pe_k.shape[0]
    wk = jnp.concatenate([w1_k[:, :CMP_STRIDE], w1_k[:, CMP_STRIDE:]], axis=-1)
    zv = jnp.zeros((d, CMP_STRIDE, B_DV, 128 - B_DV), F32)
    wv = jnp.concatenate([w1_v[:, :CMP_STRIDE], zv, w1_v[:, CMP_STRIDE:], zv], axis=-1)
    wv = jnp.pad(wv, ((0, 0), (0, 0), (0, 128 - B_DV), (0, 0)))
    return dict(pek=pe_k, pev=jnp.pad(pe_v, ((0, 0), (0, 0), (0, 128 - B_DV))),
                wk=wk.astype(BF16), wv=wv.astype(BF16), w2k=w2_k.astype(BF16),
                w2v=jnp.pad(w2_v, ((0, 0), (0, 128 - B_DV), (0, 128 - B_DV))).astype(BF16))


def _rope_table(pos):
    half = C_ROPE // 2
    inv = ROPE_THETA ** (-jnp.arange(half, dtype=F32) / half)
    ang = pos.astype(F32)[:, None] * inv
    cos, sin = jnp.cos(ang), jnp.sin(ang)
    return jnp.concatenate([cos, cos, -sin, sin], axis=-1)


def _cover(n_c, n_c_pad, n_s, n_s_pad):
    ci = np.arange(n_c)[:, None]
    sj = np.arange(n_s)[None, :]
    cov = ((ci * CMP_STRIDE < (sj + 1) * SEL_BLOCK) & (ci * CMP_STRIDE + CMP_LEN > sj * SEL_BLOCK)).astype(np.float32)
    out = np.zeros((n_c_pad, n_s_pad), np.float32)
    out[:n_c, :n_s] = cov
    return jnp.asarray(out, BF16)


def _expand(n_slots, key0, n_keys, slot0=0):
    j = np.arange(n_slots)[:, None] + slot0
    k = (np.arange(n_keys)[None, :] + key0) // SEL_BLOCK
    return (j == k).astype(np.float32)


def kernel(x_prompt, x_sample, cache_mla, cache_nsa_cmp, cache_nsa_slc, state_nsa_win, page_table, pre_norm_g, w_in, a_ln_g, a_ln_b, a_ws, a_bs, b_cmp_pe_k, b_cmp_w1_k, b_cmp_w2_k, b_cmp_pe_v, b_cmp_w1_v, b_cmp_w2_v, c_q_norm_g, c_kv_norm_g, c_w_uq, c_w_uk, c_w_uv, w_proj_a, w_proj_b, w_proj_c, w_out, post_norm_g):
    nbp, tp, _ = x_prompt.shape
    nbs, ts, _ = x_sample.shape
    n_pages = page_table.shape[1]
    past = n_pages * PAGE
    mp = nbp * tp
    ms = nbs * ts
    depth = w_in.shape[0]
    wb = state_nsa_win.shape[2]

    w_in_p = _pack_w_in(w_in)
    w_uq_p = _pack_w_uq(c_w_uq)
    w_ukt = jnp.transpose(c_w_uk, (0, 2, 3, 1)).astype(BF16)
    w_uv = jnp.transpose(c_w_uv, (0, 2, 1, 3)).astype(BF16)
    cw = _pack_cmp(b_cmp_pe_k, b_cmp_w1_k, b_cmp_w2_k, b_cmp_pe_v, b_cmp_w1_v, b_cmp_w2_v)
    wpa, wpb, wpc, wo = (w.astype(BF16) for w in (w_proj_a, w_proj_b, w_proj_c, w_out))
    tab = jnp.concatenate([jnp.tile(_rope_table(jnp.arange(tp)), (nbp, 1)),
                           jnp.tile(_rope_table(past + jnp.arange(ts)), (nbs, 1))], axis=0)
    bs_p = jnp.repeat(jnp.transpose(a_bs, (0, 2, 1)), A_WIDTH // A_GROUPS, axis=-1)
    reps = A_CHUNK // ts
    ws_s = jnp.tile(a_ws[:, :, :ts, :ts], (1, 1, reps, reps))
    bs_s = jnp.tile(bs_p[:, :ts], (1, reps, 1))

    ncp = tp // CMP_STRIDE
    cover_p = _cover(ncp - N_HALF + 1, ncp, tp // SEL_BLOCK, 128)
    emat_p = jnp.asarray(_expand(128, 0, tp), BF16)
    ncs = -(-(past + ts) // CMP_STRIDE) - N_HALF + 1
    nss = -(-(past + ts) // SEL_BLOCK)
    cover_s = _cover(ncs, ncs, nss, 256)
    chunk = 2048
    emat_s = jnp.asarray(np.stack([_expand(128, c * chunk, chunk) for c in range(past // chunk)]), BF16)
    etail_s = jnp.asarray(_expand(128, past, PAGE, slot0=128), BF16)
    rr = np.arange(B_HEADS * ts)
    gsum = jnp.asarray((rr[:, None] % ts == rr[None, :] % ts).astype(np.float32), BF16)

    x = jnp.concatenate([x_prompt.reshape(mp, D_MODEL), x_sample.reshape(ms, D_MODEL)], axis=0)
    outs = {k: [] for k in ("mla_p", "mla_s", "cmp_p", "cmp_s", "slc_p", "slc_s", "win_p", "win_s", "av_s")}
    for l in range(depth):
        cwl = {k: v[l] for k, v in cw.items()}
        proj = _inproj(x, pre_norm_g[l][None], w_in_p[l])
        ya_p = _mixa(proj, 0, mp, a_ln_g[l][None], a_ln_b[l][None], a_ws[l], bs_p[l], sample=False)[0]
        ya_s, vn_s = _mixa(proj, mp, ms, a_ln_g[l][None], a_ln_b[l][None], ws_s[l], bs_s[l], sample=True)
        qcat, ckr = _mlaprep(proj, tab, c_q_norm_g[l][None], c_kv_norm_g[l][None], w_uq_p[l], w_ukt[l])
        kc_p, vc_p = _cmp_prompt(proj, nbp, tp, cwl)
        ob_p = _nsa_prompt(proj, kc_p, vc_p, cover_p, emat_p, nbp, tp)
        olat_p = _mla_prompt(qcat, ckr, nbp, tp)
        olat_s = _mla_sample(page_table, qcat, ckr, cache_mla, l, mp, nbs, ts, chunk)
        ocmp_s, sel_s = _cmp_sample(page_table, proj, cache_nsa_cmp, cwl, cover_s, gsum, l, mp, nbs, ts)
        ob_s, nwin_s = _slc_sample(page_table, proj, sel_s, ocmp_s, state_nsa_win, emat_s, etail_s,
                                   cache_nsa_slc, l, mp, nbs, ts, chunk)
        ya = jnp.concatenate([ya_p, ya_s], axis=0)
        ob = jnp.concatenate([ob_p, ob_s], axis=0)
        olat = jnp.concatenate([olat_p, olat_s], axis=1)
        merged = _merge(ya, ob, olat, proj, w_uv[l], wpa[l], wpb[l], wpc[l])
        x = _outproj(merged, x, wo[l], post_norm_g[l][None])

        rows = lambda off: proj[:, OFF_BM + off:OFF_BM + off + ROW_B]
        cmp_r, slc_r, win_r = rows(BM_CMP), rows(BM_SLC), rows(BM_WIN)
        outs["mla_p"].append(ckr[:mp].reshape(nbp, tp, ROW_C))
        outs["mla_s"].append(ckr[mp:].reshape(nbs, ts, ROW_C))
        outs["cmp_p"].append(cmp_r[:mp].reshape(nbp, tp, ROW_B))
        outs["cmp_s"].append(cmp_r[mp:].reshape(nbs, ts, ROW_B))
        outs["slc_p"].append(slc_r[:mp].reshape(nbp, tp, ROW_B))
        outs["slc_s"].append(slc_r[mp:].reshape(nbs, ts, ROW_B))
        outs["win_p"].append(win_r[:mp].reshape(nbp, tp, ROW_B)[:, -min(WINDOW, tp):])
        outs["win_s"].append(nwin_s)
        outs["av_s"].append(vn_s.reshape(nbs, ts, A_WIDTH))

    st = lambda k: jnp.stack(outs[k], axis=1)
    return (x[:mp].reshape(nbp, tp, D_MODEL), x[mp:].reshape(nbs, ts, D_MODEL),
            st("mla_p"), st("mla_s"), st("cmp_p"), st("cmp_s"), st("slc_p"), st("slc_s"),
            st("win_p"), st("win_s"), st("av_s"))
```

```python
import functools

import numpy as np
import jax
import jax.numpy as jnp
from jax import lax
from jax.experimental import pallas as pl
from jax.experimental.pallas import tpu as pltpu

F32 = jnp.float32
BF16 = jnp.bfloat16

D_MODEL = 2048
PAGE = 128
A_WIDTH = 1024
A_GROUPS = 8
A_CHUNK = 128
B_HEADS = 16
B_DK = 128
B_DV = 64
CMP_LEN = 32
CMP_STRIDE = 16
SEL_BLOCK = 64
N_SELECT = 16
WINDOW = 512
FORCE_BONUS = 1e4
C_HEADS = 8
C_Q_LORA = 512
C_KV_LORA = 128
C_NOPE = 128
C_ROPE = 64
ROPE_THETA = 10000.0
Q_BLOCK = 128
NORM_EPS = 1e-6
ROW_B = B_DK + B_DV
ROW_C = C_KV_LORA + C_ROPE
SLOT = 256
N_HALF = CMP_LEN // CMP_STRIDE

IN_WIDTHS = (2 * A_WIDTH, A_WIDTH, B_HEADS * B_DK, B_DK, B_DV, B_DK, B_DV, B_DK, B_DV, 3 * B_HEADS,
             B_HEADS * B_DV, C_Q_LORA, C_KV_LORA + C_ROPE, C_HEADS * 128, 3 * D_MODEL)
SPLIT_AT = tuple(int(s) for s in np.cumsum(IN_WIDTHS)[:-1])

OFF_GATES, OFF_AUV, OFF_BQ, OFF_AZ, OFF_BM, OFF_BZ, OFF_CM, OFF_CZ = (
    0, 6144, 8192, 10240, 11264, 12288, 13312, 14336)
PW = 15360
BM_CMP, BM_SLC, BM_WIN, BM_G = 0, 256, 512, 768
CM_DQ, CM_KV, CM_KR = 0, 512, 640

SCALE_B = B_DK ** -0.5
SCALE_C = (C_NOPE + C_ROPE) ** -0.5
NEG = -1e30
LOG2E = 1.4426950408889634
VMEM_LIMIT = 56 * 1024 * 1024


def _nt(a, b):
    return lax.dot_general(a, b, (((1,), (1,)), ((), ())), preferred_element_type=F32)


def _nn(a, b):
    return jnp.dot(a, b, preferred_element_type=F32)


def _split3(x):
    h = x.astype(BF16)
    r = x - h.astype(F32)
    m = r.astype(BF16)
    lo = (r - m.astype(F32)).astype(BF16)
    return h, m, lo


def _nn_exact_rhs01(x, w01):
    h, m, lo = _split3(x)
    return _nn(h, w01) + _nn(m, w01) + _nn(lo, w01)


def _nn_exact_lhs01(w01, x):
    h, m, lo = _split3(x)
    return _nn(w01, h) + _nn(w01, m) + _nn(w01, lo)


def _log2(n):
    assert n & (n - 1) == 0, n
    return n.bit_length() - 1


def _div(x, n):
    return jnp.right_shift(x, _log2(n))


def _mod(x, n):
    return jnp.bitwise_and(x, n - 1)


def _params(*sem):
    return pltpu.CompilerParams(dimension_semantics=sem, vmem_limit_bytes=VMEM_LIMIT)


def _rms(x, g):
    return x * lax.rsqrt(jnp.mean(x * x, axis=-1, keepdims=True) + NORM_EPS) * g


def _silu(x):
    return x * jax.nn.sigmoid(x)


def _inproj_kernel(x_ref, g_ref, w_ref, o_ref, h_ref):
    @pl.when(pl.program_id(1) == 0)
    def _():
        h_ref[...] = _rms(x_ref[...], g_ref[...]).astype(BF16)

    o_ref[...] = _nn(h_ref[...], w_ref[...])


def _inproj(x, g, w, tm=1024, tn=1024):
    m = x.shape[0]
    return pl.pallas_call(
        _inproj_kernel,
        grid=(m // tm, PW // tn),
        in_specs=[pl.BlockSpec((tm, D_MODEL), lambda i, j: (i, 0)),
                  pl.BlockSpec((1, D_MODEL), lambda i, j: (0, 0)),
                  pl.BlockSpec((D_MODEL, tn), lambda i, j: (0, j))],
        out_specs=pl.BlockSpec((tm, tn), lambda i, j: (i, j)),
        out_shape=jax.ShapeDtypeStruct((m, PW), F32),
        scratch_shapes=[pltpu.VMEM((tm, D_MODEL), BF16)],
        compiler_params=_params("arbitrary", "arbitrary"),
        name="in_proj",
    )(x, g, w)


def _mixa_kernel(auv_ref, az_ref, lng_ref, lnb_ref, ws_ref, bs_ref, y_ref, *vn_out, sample):
    auv = jax.nn.gelu(auv_ref[...])
    u = auv[:, :A_WIDTH]
    v = auv[:, A_WIDTH:]
    mu = jnp.mean(v, axis=-1, keepdims=True)
    var = jnp.mean(jnp.square(v - mu), axis=-1, keepdims=True)
    vn = (v - mu) * lax.rsqrt(var + NORM_EPS) * lng_ref[...] + lnb_ref[...]
    if vn_out:
        vn_out[0][...] = vn
    r = lax.broadcasted_iota(jnp.int32, (A_CHUNK, A_CHUNK), 0)
    c = lax.broadcasted_iota(jnp.int32, (A_CHUNK, A_CHUNK), 1)
    mask = c <= r
    if sample:
        mask = mask & (_div(r, 8) == _div(c, 8))
    vb = vn.astype(BF16)
    gw = A_WIDTH // A_GROUPS
    for g in range(A_GROUPS):
        w = jnp.where(mask, ws_ref[g], 0.0).astype(BF16)
        s = _nn(w, vb[:, g * gw:(g + 1) * gw]) + bs_ref[:, g * gw:(g + 1) * gw]
        y = u[:, g * gw:(g + 1) * gw] * s * _silu(az_ref[:, g * gw:(g + 1) * gw])
        y_ref[:, g * gw:(g + 1) * gw] = y.astype(BF16)


def _mixa(proj, row0, nrows, lng, lnb, ws, bs, sample):
    rb0 = row0 // A_CHUNK
    nb = nrows // A_CHUNK
    out_shape = [jax.ShapeDtypeStruct((nrows, A_WIDTH), BF16)]
    out_specs = [pl.BlockSpec((A_CHUNK, A_WIDTH), lambda i: (i, 0))]
    if sample:
        out_shape.append(jax.ShapeDtypeStruct((nrows, A_WIDTH), F32))
        out_specs.append(pl.BlockSpec((A_CHUNK, A_WIDTH), lambda i: (i, 0)))
    return pl.pallas_call(
        functools.partial(_mixa_kernel, sample=sample),
        grid=(nb,),
        in_specs=[pl.BlockSpec((A_CHUNK, 2 * A_WIDTH), lambda i: (rb0 + i, OFF_AUV // (2 * A_WIDTH))),
                  pl.BlockSpec((A_CHUNK, A_WIDTH), lambda i: (rb0 + i, OFF_AZ // A_WIDTH)),
                  pl.BlockSpec((1, A_WIDTH), lambda i: (0, 0)),
                  pl.BlockSpec((1, A_WIDTH), lambda i: (0, 0)),
                  pl.BlockSpec((A_GROUPS, A_CHUNK, A_CHUNK), lambda i: (0, 0, 0)),
                  pl.BlockSpec((A_CHUNK, A_WIDTH), lambda i: (0, 0))],
        out_specs=out_specs,
        out_shape=out_shape,
        compiler_params=_params("arbitrary"),
        name="mix_a_sample" if sample else "mix_a_prompt",
    )(proj, proj, lng, lnb, ws, bs)


def _mlaprep_kernel(cm_ref, tab_ref, qg_ref, kvg_ref, wuq_ref, wukt_ref, qcat_ref, ckr_ref):
    tab = tab_ref[...]
    h = _rms(cm_ref[:, CM_DQ:CM_DQ + C_Q_LORA], qg_ref[...]).astype(BF16)
    q = _nn(h, wuq_ref[...])
    lo = lax.broadcasted_iota(jnp.int32, tab.shape, 1) < C_ROPE
    for hh in range(C_HEADS):
        qn = q[:, hh * C_NOPE:(hh + 1) * C_NOPE].astype(BF16)
        qcat_ref[hh, :, 0:C_KV_LORA] = _nn(qn, wukt_ref[hh])
        pr = q[:, C_HEADS * C_NOPE + hh * 128:C_HEADS * C_NOPE + (hh + 1) * 128] * tab
        ro = pr + pltpu.roll(pr, 64, 1)
        qcat_ref[hh, :, C_KV_LORA:SLOT] = jnp.where(lo, ro, 0.0)
    ckr_ref[:, 0:C_KV_LORA] = _rms(cm_ref[:, CM_KV:CM_KV + C_KV_LORA], kvg_ref[...])
    prk = cm_ref[:, CM_KR:CM_KR + 128] * tab
    rok = prk + pltpu.roll(prk, 64, 1)
    ckr_ref[:, C_KV_LORA:ROW_C] = rok[:, 0:C_ROPE]


def _mlaprep(proj, tab, qg, kvg, wuq, wukt, tm=256):
    m = proj.shape[0]
    return pl.pallas_call(
        _mlaprep_kernel,
        grid=(m // tm,),
        in_specs=[pl.BlockSpec((tm, 1024), lambda i: (i, OFF_CM // 1024)),
                  pl.BlockSpec((tm, 128), lambda i: (i, 0)),
                  pl.BlockSpec((1, C_Q_LORA), lambda i: (0, 0)),
                  pl.BlockSpec((1, C_KV_LORA), lambda i: (0, 0)),
                  pl.BlockSpec((C_Q_LORA, 2048), lambda i: (0, 0)),
                  pl.BlockSpec((C_HEADS, C_NOPE, C_KV_LORA), lambda i: (0, 0, 0))],
        out_specs=[pl.BlockSpec((C_HEADS, tm, SLOT), lambda i: (0, i, 0)),
                   pl.BlockSpec((tm, ROW_C), lambda i: (i, 0))],
        out_shape=[jax.ShapeDtypeStruct((C_HEADS, m, SLOT), F32),
                   jax.ShapeDtypeStruct((m, ROW_C), F32)],
        compiler_params=_params("arbitrary"),
        name="mla_prep",
    )(proj, tab, qg, kvg, wuq, wukt)


def _attend_many(chains):
    ss = [_nt(q, k) + b for q, k, b, _ in chains]
    es = [jnp.exp2(s - jnp.max(s, axis=-1, keepdims=True)) for s in ss]
    return [_nn(e.astype(BF16), c[3]) / jnp.sum(e, axis=-1, keepdims=True) for e, c in zip(es, chains)]


def _causal_widths(t, step):
    return [w for w in range(step, t + step, step)]


def _mla_prompt_kernel(q_ref, k_ref, o_ref, kbf_ref, qs_ref, bias_ref, *, step):
    i = pl.program_id(1)
    t = k_ref.shape[0]

    @pl.when(i == 0)
    def _():
        kbf_ref[...] = jnp.zeros_like(kbf_ref)
        kbf_ref[:, 0:C_KV_LORA] = k_ref[:, 0:C_KV_LORA].astype(BF16)
        kbf_ref[:, C_KV_LORA:ROW_C] = k_ref[:, C_KV_LORA:ROW_C].astype(BF16)

    for h in range(C_HEADS):
        qs_ref[h] = (q_ref[h] * (SCALE_C * LOG2E)).astype(BF16)
    qpos = i * Q_BLOCK + lax.broadcasted_iota(jnp.int32, (Q_BLOCK, t), 0)
    kpos = lax.broadcasted_iota(jnp.int32, (Q_BLOCK, t), 1)
    bias_ref[...] = jnp.where(kpos <= qpos, 0.0, NEG)

    for w in _causal_widths(t, step):
        @pl.when((i * Q_BLOCK) // step == w // step - 1)
        def _(w=w):
            def heads(g, carry):
                hs = [g * 4 + j for j in range(4)]
                outs = _attend_many([(qs_ref[h], kbf_ref[0:w, :], bias_ref[:, 0:w], kbf_ref[0:w, 0:C_KV_LORA])
                                     for h in hs])
                for h, o in zip(hs, outs):
                    o_ref[h] = o
                return carry
            lax.fori_loop(0, C_HEADS // 4, heads, 0)


def _mla_prompt(qcat, ckr, nb, t, step=512):
    nq = t // Q_BLOCK
    return pl.pallas_call(
        functools.partial(_mla_prompt_kernel, step=step),
        grid=(nb, nq),
        in_specs=[pl.BlockSpec((C_HEADS, Q_BLOCK, SLOT), lambda b, i: (0, b * nq + i, 0)),
                  pl.BlockSpec((t, ROW_C), lambda b, i: (b, 0))],
        out_specs=pl.BlockSpec((C_HEADS, Q_BLOCK, C_KV_LORA), lambda b, i: (0, b * nq + i, 0)),
        out_shape=jax.ShapeDtypeStruct((C_HEADS, nb * t, C_KV_LORA), F32),
        scratch_shapes=[pltpu.VMEM((t, SLOT), BF16), pltpu.VMEM((C_HEADS, Q_BLOCK, SLOT), BF16),
                        pltpu.VMEM((Q_BLOCK, t), F32)],
        compiler_params=_params("arbitrary", "arbitrary"),
        name="mla_prompt",
    )(qcat, ckr)


def _pe_bias(pek_ref, pev_ref, wk_ref, wv_ref):
    pk = pek_ref[...].astype(BF16)
    pv = pev_ref[...].astype(BF16)
    bk = jnp.zeros((1, 128), F32)
    bv = jnp.zeros((1, 128), F32)
    for l in range(CMP_STRIDE):
        fk = _nn(pk, wk_ref[l])
        fv = _nn(pv, wv_ref[l])
        bk = bk + fk[l:l + 1, 0:128] + fk[CMP_STRIDE + l:CMP_STRIDE + l + 1, 128:256]
        bv = bv + fv[l:l + 1, 0:128] + fv[CMP_STRIDE + l:CMP_STRIDE + l + 1, 128:256]
    return bk, bv


def _compress(k_at, v_at, n_ch, n_out, bk, bv, wk_ref, wv_ref, w2k_ref, w2v_ref, acck_ref, accv_ref):
    acck = jnp.zeros((n_ch, 2 * B_DK), F32)
    accv = jnp.zeros((n_ch, 2 * 128), F32)
    for l in range(CMP_STRIDE):
        acck = acck + _nn(k_at(l).astype(BF16), wk_ref[l])
        accv = accv + _nn(v_at(l).astype(BF16), wv_ref[l])
    acck_ref[...] = acck
    accv_ref[...] = accv
    hk = acck_ref[0:n_out, 0:B_DK] + acck_ref[1:n_out + 1, B_DK:2 * B_DK] + bk
    hv = accv_ref[0:n_out, 0:128] + accv_ref[1:n_out + 1, 128:256] + bv
    kc = _nn(_silu(hk).astype(BF16), w2k_ref[...])
    vc = _nn(_silu(hv).astype(BF16), w2v_ref[...])
    return kc, vc


def _topk_mask(sc, nsel):
    nb, r = sc.shape
    idx = lax.broadcasted_iota(jnp.int32, (nb, r), 0).astype(F32)

    def body(_, carry):
        sc, sel = carry
        m = jnp.max(sc, axis=0, keepdims=True)
        ism = (sc == m) & (m > -jnp.inf)
        first = jnp.min(jnp.where(ism, idx, float(nb)), axis=0, keepdims=True)
        pick = idx == first
        return jnp.where(pick, -jnp.inf, sc), jnp.where(pick, 1.0, sel)

    _, sel = lax.fori_loop(0, nsel, body, (sc, jnp.zeros((nb, r), F32)))
    return sel


def _block_scores(imp, qpos):
    jj = lax.broadcasted_iota(jnp.int32, imp.shape, 1)
    cur = _div(qpos, SEL_BLOCK)
    valid = jj * SEL_BLOCK <= qpos
    forced = (jj == 0) | (jj == cur) | (jj == cur - 1)
    return jnp.where(valid, imp + jnp.where(forced, FORCE_BONUS, 0.0), -jnp.inf)


def _cmp_prompt_kernel(bm_ref, pek_ref, pev_ref, wk_ref, wv_ref, w2k_ref, w2v_ref, kc_ref, vc_ref,
                       ks_ref, vs_ref, acck_ref, accv_ref, *, t, n_ch, n_out):
    ks_ref[...] = jnp.zeros_like(ks_ref)
    vs_ref[...] = jnp.zeros_like(vs_ref)
    ks_ref[0:t, :] = bm_ref[:, BM_CMP:BM_CMP + B_DK]
    vs_ref[0:t, :] = bm_ref[:, BM_CMP + B_DK:BM_CMP + SLOT]
    bk, bv = _pe_bias(pek_ref, pev_ref, wk_ref, wv_ref)
    kc, vc = _compress(lambda l: ks_ref[pl.ds(l, n_ch, stride=CMP_STRIDE), :],
                       lambda l: vs_ref[pl.ds(l, n_ch, stride=CMP_STRIDE), :],
                       n_ch, n_out, bk, bv, wk_ref, wv_ref, w2k_ref, w2v_ref, acck_ref, accv_ref)
    kc_ref[...] = kc
    vc_ref[...] = vc


def _cmp_prompt(proj, nb, t, cw):
    n_out = t // CMP_STRIDE
    n_ch = n_out + 8
    rows = n_ch * CMP_STRIDE
    return pl.pallas_call(
        functools.partial(_cmp_prompt_kernel, t=t, n_ch=n_ch, n_out=n_out),
        grid=(nb,),
        in_specs=[pl.BlockSpec((t, 1024), lambda b: (b, OFF_BM // 1024)),
                  pl.BlockSpec((CMP_LEN, 128), lambda b: (0, 0)),
                  pl.BlockSpec((CMP_LEN, 128), lambda b: (0, 0)),
                  pl.BlockSpec((CMP_STRIDE, 128, 256), lambda b: (0, 0, 0)),
                  pl.BlockSpec((CMP_STRIDE, 128, 256), lambda b: (0, 0, 0)),
                  pl.BlockSpec((128, 128), lambda b: (0, 0)),
                  pl.BlockSpec((128, 128), lambda b: (0, 0))],
        out_specs=[pl.BlockSpec((None, n_out, 128), lambda b: (b, 0, 0)),
                   pl.BlockSpec((None, n_out, 128), lambda b: (b, 0, 0))],
        out_shape=[jax.ShapeDtypeStruct((nb, n_out, 128), F32),
                   jax.ShapeDtypeStruct((nb, n_out, 128), F32)],
        scratch_shapes=[pltpu.VMEM((rows, 128), F32), pltpu.VMEM((rows, 128), F32),
                        pltpu.VMEM((n_ch, 256), F32), pltpu.VMEM((n_ch, 256), F32)],
        compiler_params=_params("arbitrary"),
        name="cmp_prompt",
    )(proj, cw["pek"], cw["pev"], cw["wk"], cw["wv"], cw["w2k"], cw["w2v"])


def _nsa_prompt_kernel(q_ref, g_ref, bm_ref, kc_ref, vc_ref, cover_ref, eneg_ref, gsel_ref, o_ref,
                       qa_ref, kaug_ref, slv_ref, wk_ref, wv_ref, kcb_ref, vcb_ref,
                       ocmp_ref, gsp_ref, bias_s_ref, bias_w_ref, opair_ref, *, t, step):
    i = pl.program_id(1)
    n_pair = B_HEADS // 2
    wrows = WINDOW + Q_BLOCK

    @pl.when(i == 0)
    def _():
        kaug_ref[:, 0:B_DK] = bm_ref[:, BM_SLC:BM_SLC + B_DK].astype(BF16)
        kaug_ref[:, B_DK:2 * B_DK] = eneg_ref[...]
        v = bm_ref[:, BM_SLC + B_DK:BM_SLC + SLOT]
        slv_ref[0] = v.astype(BF16)
        slv_ref[1] = pltpu.roll(v, 64, 1).astype(BF16)
        wk_ref[...] = bm_ref[:, BM_WIN:BM_WIN + B_DK].astype(BF16)
        v = bm_ref[:, BM_WIN + B_DK:BM_WIN + SLOT]
        wv_ref[0] = v.astype(BF16)
        wv_ref[1] = pltpu.roll(v, 64, 1).astype(BF16)
        kcb_ref[...] = kc_ref[...].astype(BF16)
        vcb_ref[0] = vc_ref[...].astype(BF16)
        vcb_ref[1] = pltpu.roll(vc_ref[...], 64, 1).astype(BF16)

    for h in range(B_HEADS):
        qa_ref[h, :, 0:B_DK] = (q_ref[:, h * B_DK:(h + 1) * B_DK] * (SCALE_B * LOG2E)).astype(BF16)

    gs = jax.nn.sigmoid(g_ref[:, BM_G:BM_G + 128])
    g_hi = gs.astype(BF16)
    g_lo = (gs - g_hi.astype(F32)).astype(BF16)
    gall = _nn(g_hi, gsel_ref[...]) + _nn(g_lo, gsel_ref[...])
    for c in range(3 * n_pair):
        gsp_ref[c] = gall[:, c * 128:(c + 1) * 128]

    qpos1 = i * Q_BLOCK + lax.broadcasted_iota(jnp.int32, (Q_BLOCK, 1), 0)

    n_c = kcb_ref.shape[0]
    c_end = lax.broadcasted_iota(jnp.int32, (Q_BLOCK, n_c), 1) * CMP_STRIDE + (CMP_LEN - 1)
    c_ok = c_end <= qpos1
    rows = B_HEADS * Q_BLOCK
    s = _nt(qa_ref[:, :, 0:B_DK].reshape(rows, B_DK), kcb_ref[...]).reshape(B_HEADS, Q_BLOCK, n_c)
    s = jnp.where(c_ok[None], s, NEG)
    e = jnp.where(c_ok[None], jnp.exp2(s - jnp.max(s, axis=-1, keepdims=True)), 0.0)
    pr = e / jnp.maximum(jnp.sum(e, axis=-1, keepdims=True), 1e-30)
    psum = jnp.sum(pr, axis=0)
    prb = pr.astype(BF16).reshape(n_pair, 2, Q_BLOCK, n_c)
    ocmp_ref[...] = (_nn(prb[:, 0].reshape(n_pair * Q_BLOCK, n_c), vcb_ref[0])
                     + _nn(prb[:, 1].reshape(n_pair * Q_BLOCK, n_c), vcb_ref[1])).reshape(n_pair, Q_BLOCK, 128)

    imp = _nn_exact_rhs01(psum, cover_ref[...])
    score = _block_scores(imp, qpos1)
    n_s = t // SEL_BLOCK
    sel_t = _topk_mask(score.T[0:n_s, :], min(N_SELECT, n_s))
    sel = jnp.concatenate([sel_t, jnp.zeros((128 - n_s, Q_BLOCK), F32)], axis=0).T
    unsel = (1.0 - sel).astype(BF16)
    for h in range(B_HEADS):
        qa_ref[h, :, B_DK:2 * B_DK] = unsel

    kpos = lax.broadcasted_iota(jnp.int32, (Q_BLOCK, t), 1)
    bias_s_ref[...] = jnp.where(kpos <= qpos1, 0.0, NEG)
    ws = pl.multiple_of(jnp.maximum(i * Q_BLOCK - WINDOW, 0), Q_BLOCK)
    wpos = ws + lax.broadcasted_iota(jnp.int32, (Q_BLOCK, wrows), 1)
    bias_w_ref[...] = jnp.where((wpos <= qpos1) & (wpos >= qpos1 - WINDOW), 0.0, NEG)

    for w in _causal_widths(t, step):
        @pl.when((i * Q_BLOCK) // step == w // step - 1)
        def _(w=w):
            def pair(p, carry):
                chains = []
                for hh in range(2):
                    h = 2 * p + hh
                    chains.append((qa_ref[h], kaug_ref[0:w, :], bias_s_ref[:, 0:w], slv_ref[hh, 0:w, :]))
                    chains.append((qa_ref[h, :, 0:B_DK], wk_ref[pl.ds(ws, wrows), :], bias_w_ref[...],
                                   wv_ref[hh, pl.ds(ws, wrows), :]))
                se, we, so, wo = _attend_many(chains)
                opair_ref[p] = gsp_ref[3 * p] * ocmp_ref[p] + gsp_ref[3 * p + 1] * (se + so) + gsp_ref[3 * p + 2] * (we + wo)
                return carry
            lax.fori_loop(0, n_pair, pair, 0)

    for p in range(n_pair):
        o_ref[:, p * 128:(p + 1) * 128] = opair_ref[p]


def _nsa_prompt(proj, kc, vc, cover, eneg, gsel, nb, t, step=512):
    nq = t // Q_BLOCK
    n_c = kc.shape[1]
    n_pair = B_HEADS // 2
    return pl.pallas_call(
        functools.partial(_nsa_prompt_kernel, t=t, step=step),
        grid=(nb, nq),
        in_specs=[pl.BlockSpec((Q_BLOCK, B_HEADS * B_DK), lambda b, i: (b * nq + i, OFF_BQ // (B_HEADS * B_DK))),
                  pl.BlockSpec((Q_BLOCK, 1024), lambda b, i: (b * nq + i, OFF_BM // 1024)),
                  pl.BlockSpec((t, 1024), lambda b, i: (b, OFF_BM // 1024)),
                  pl.BlockSpec((None, n_c, 128), lambda b, i: (b, 0, 0)),
                  pl.BlockSpec((None, n_c, 128), lambda b, i: (b, 0, 0)),
                  pl.BlockSpec((n_c, 128), lambda b, i: (0, 0)),
                  pl.BlockSpec((t, 128), lambda b, i: (0, 0)),
                  pl.BlockSpec((128, 3 * n_pair * 128), lambda b, i: (0, 0))],
        out_specs=pl.BlockSpec((Q_BLOCK, B_HEADS * B_DV), lambda b, i: (b * nq + i, 0)),
        out_shape=jax.ShapeDtypeStruct((nb * t, B_HEADS * B_DV), F32),
        scratch_shapes=[pltpu.VMEM((B_HEADS, Q_BLOCK, 2 * B_DK), BF16),
                        pltpu.VMEM((t, 2 * B_DK), BF16), pltpu.VMEM((2, t, 128), BF16),
                        pltpu.VMEM((t, B_DK), BF16), pltpu.VMEM((2, t, 128), BF16),
                        pltpu.VMEM((n_c, B_DK), BF16), pltpu.VMEM((2, n_c, 128), BF16),
                        pltpu.VMEM((n_pair, Q_BLOCK, 128), F32),
                        pltpu.VMEM((3 * n_pair, Q_BLOCK, 128), F32),
                        pltpu.VMEM((Q_BLOCK, t), F32),
                        pltpu.VMEM((Q_BLOCK, WINDOW + Q_BLOCK), F32),
                        pltpu.VMEM((n_pair, Q_BLOCK, 128), F32)],
        compiler_params=_params("arbitrary", "arbitrary"),
        name="nsa_prompt",
    )(proj, proj, proj, kc, vc, cover, eneg, gsel)


def _page_copy(pt_ref, cache_ref, layer, bb, slot, buf, sem):
    def at(j):
        return pltpu.make_async_copy(cache_ref.at[pt_ref[bb, j], layer], buf.at[slot, :, pl.ds(j * PAGE, PAGE)],
                                     sem.at[slot])
    return at


def _gather_step(pt_ref, cache_ref, layer, n_pages, buf, sem, init):
    b = pl.program_id(0)
    nb = pl.num_programs(0)
    slot = lax.rem(b, 2)

    def start(bb, sl):
        at = _page_copy(pt_ref, cache_ref, layer, bb, sl, buf, sem)

        def body(j, c):
            at(j).start()
            return c
        lax.fori_loop(0, n_pages, body, 0, unroll=8)

    @pl.when(b == 0)
    def _():
        init()
        start(0, 0)

    @pl.when(b + 1 < nb)
    def _():
        start(b + 1, 1 - slot)

    at = _page_copy(pt_ref, cache_ref, layer, b, slot, buf, sem)

    def wbody(j, c):
        at(j).wait()
        return c
    lax.fori_loop(0, n_pages, wbody, 0, unroll=8)
    return slot


def _online_chunk(s, pv, carry):
    m, l, acc = carry
    m_new = jnp.maximum(m, jnp.max(s, axis=-1, keepdims=True))
    a = jnp.exp2(m - m_new)
    p = jnp.exp2(s - m_new)
    return m_new, a * l + jnp.sum(p, axis=-1, keepdims=True), a * acc + pv(p.astype(BF16))


def _mla_sample_kernel(pt_ref, q_ref, new_ref, cache_ref, o_ref, kbuf_ref, tail_ref, sem, *, layer, n_pages, ts, chunk):
    slot = _gather_step(pt_ref, cache_ref, layer, n_pages, kbuf_ref, sem, lambda: None)
    rows = C_HEADS * ts
    q = q_ref[...].reshape(rows, SLOT)[:, 0:ROW_C]
    qb = (q * (SCALE_C * LOG2E)).astype(BF16)
    tail_ref[...] = jnp.zeros_like(tail_ref)
    tail_ref[0:ts, :] = new_ref[...]

    carry = (jnp.full((rows, 1), NEG, F32), jnp.zeros((rows, 1), F32), jnp.zeros((rows, C_KV_LORA), F32))
    for c in range(n_pages * PAGE // chunk):
        kt = kbuf_ref[slot, :, c * chunk:(c + 1) * chunk].astype(BF16)
        carry = _online_chunk(_nn(qb, kt), lambda p: _nt(p, kt[0:C_KV_LORA, :]), carry)
    tq = _mod(lax.broadcasted_iota(jnp.int32, (rows, PAGE), 0), ts)
    tk = lax.broadcasted_iota(jnp.int32, (rows, PAGE), 1)
    kn = tail_ref[...].astype(BF16)
    _, l, acc = _online_chunk(_nt(qb, kn) + jnp.where(tk <= tq, 0.0, NEG), lambda p: _nn(p, kn[:, 0:C_KV_LORA]), carry)
    o = acc / l
    for h in range(C_HEADS):
        o_ref[h] = o[h * ts:(h + 1) * ts, :]


def _mla_sample(page_table, qcat, ckr, cache_t, layer, row0, nb, ts, chunk=2048):
    n_pages = page_table.shape[1]
    rb0 = row0 // ts
    grid_spec = pltpu.PrefetchScalarGridSpec(
        num_scalar_prefetch=1,
        grid=(nb,),
        in_specs=[pl.BlockSpec((C_HEADS, ts, SLOT), lambda b, pt: (0, rb0 + b, 0)),
                  pl.BlockSpec((ts, ROW_C), lambda b, pt: (rb0 + b, 0)),
                  pl.BlockSpec(memory_space=pl.ANY)],
        out_specs=pl.BlockSpec((C_HEADS, ts, C_KV_LORA), lambda b, pt: (0, b, 0)),
        scratch_shapes=[pltpu.VMEM((2, ROW_C, n_pages * PAGE), F32), pltpu.VMEM((PAGE, ROW_C), F32),
                        pltpu.SemaphoreType.DMA((2,))],
    )
    return pl.pallas_call(
        functools.partial(_mla_sample_kernel, layer=layer, n_pages=n_pages, ts=ts, chunk=chunk),
        grid_spec=grid_spec,
        out_shape=jax.ShapeDtypeStruct((C_HEADS, nb * ts, C_KV_LORA), F32),
        compiler_params=_params("arbitrary"),
        name="mla_sample",
    )(page_table, qcat, ckr, cache_t)


def _cmp_sample_kernel(pt_ref, q_ref, bm_ref, cache_ref, pek_ref, pevl_ref, pevh_ref, wk_ref, wv_ref, w2k_ref, w2v_ref,
                       perm_ref, cover_ref, gsum_ref, ocmp_ref, sel_ref, cbuf_ref, xk_ref, xv_ref, acck_ref, accv_ref,
                       pb_ref, sem, *, layer, n_pages, ts, n_out, past):
    grp = 2 * PAGE
    cpg = grp // CMP_STRIDE
    n_grp = past // grp

    def phase_weights(l):
        wk = wk_ref[l % 8, (l // 8) * 128:(l // 8 + 1) * 128, :]
        wv = wv_ref[l % 4, (l // 8) * 128:(l // 8 + 1) * 128, :]
        return wk, wv

    def init():
        pk = pek_ref[...].astype(BF16)
        pvl = pevl_ref[...].astype(BF16)
        pvh = pevh_ref[...].astype(BF16)
        bk = jnp.zeros((1, 128), F32)
        bv = jnp.zeros((1, 128), F32)
        for l in range(CMP_STRIDE):
            wk, wv = phase_weights(l)
            fk = _nn(pk, wk)
            fv = _nn(pvl if (l // 4) % 2 == 0 else pvh, wv)
            bk = bk + fk[l:l + 1, 0:128] + fk[CMP_STRIDE + l:CMP_STRIDE + l + 1, 128:256]
            bv = bv + fv[l:l + 1, 0:128] + fv[CMP_STRIDE + l:CMP_STRIDE + l + 1, 128:256]
        pb_ref[0:1, :] = bk
        pb_ref[1:2, :] = bv

    slot = _gather_step(pt_ref, cache_ref, layer, n_pages, cbuf_ref, sem, init)
    rows = B_HEADS * ts

    perm = perm_ref[...]
    for g in range(n_grp):
        xt = cbuf_ref[slot, :, g * grp:(g + 1) * grp].astype(BF16)
        y = _nt(perm, jnp.concatenate([xt, xt[B_DK:ROW_B]], axis=0)).astype(BF16)
        for l in range(CMP_STRIDE):
            yl = y[l * cpg:(l + 1) * cpg]
            xk_ref[l % 8, g * cpg:(g + 1) * cpg, (l // 8) * 128:(l // 8 + 1) * 128] = yl[:, 0:B_DK]
            c0 = ((l // 4) % 2) * B_DV
            xv_ref[l % 4, g * cpg:(g + 1) * cpg, (l // 4) * B_DV:(l // 4 + 1) * B_DV] = yl[:, B_DK + c0:B_DK + c0 + B_DV]
    new_k = bm_ref[:, BM_CMP:BM_CMP + B_DK]
    new_v = bm_ref[:, BM_CMP + B_DK:BM_CMP + SLOT]
    new_vh = pltpu.roll(new_v, 64, 1)
    first = lax.broadcasted_iota(jnp.int32, (cpg, 128), 0) == 0
    c0 = n_grp * cpg
    for l in range(CMP_STRIDE):
        if l < ts:
            tk = jnp.where(first, jnp.broadcast_to(new_k[l:l + 1], (cpg, 128)), 0.0)
            nv = new_v if (l // 4) % 2 == 0 else new_vh
            tv = jnp.where(first, jnp.broadcast_to(nv[l:l + 1], (cpg, 128)), 0.0)
        else:
            tk = tv = jnp.zeros((cpg, 128), F32)
        xk_ref[l % 8, c0:c0 + cpg, (l // 8) * 128:(l // 8 + 1) * 128] = tk.astype(BF16)
        c1 = ((l // 4) % 2) * B_DV
        xv_ref[l % 4, c0:c0 + cpg, (l // 4) * B_DV:(l // 4 + 1) * B_DV] = tv[:, c1:c1 + B_DV].astype(BF16)

    acck = _nn(xk_ref[0], wk_ref[0])
    for j in range(1, 8):
        acck = acck + _nn(xk_ref[j], wk_ref[j])
    accv = _nn(xv_ref[0], wv_ref[0])
    for j in range(1, 4):
        accv = accv + _nn(xv_ref[j], wv_ref[j])
    acck_ref[...] = acck
    accv_ref[...] = accv
    hk = acck_ref[0:n_out, 0:B_DK] + acck_ref[1:n_out + 1, B_DK:2 * B_DK] + pb_ref[0:1, :]
    hv = accv_ref[0:n_out, 0:128] + accv_ref[1:n_out + 1, 128:256] + pb_ref[1:2, :]
    kc = _nn(_silu(hk).astype(BF16), w2k_ref[...])
    vc = _nn(_silu(hv).astype(BF16), w2v_ref[...])

    q = jnp.concatenate([q_ref[:, h * B_DK:(h + 1) * B_DK] for h in range(B_HEADS)], axis=0)
    qb = (q * (SCALE_B * LOG2E)).astype(BF16)
    qpos = past + _mod(lax.broadcasted_iota(jnp.int32, (rows, 1), 0), ts)
    c_end = lax.broadcasted_iota(jnp.int32, (rows, n_out), 1) * CMP_STRIDE + (CMP_LEN - 1)
    c_ok = c_end <= qpos
    s = jnp.where(c_ok, _nt(qb, kc.astype(BF16)), NEG)
    m = jnp.max(s, axis=-1, keepdims=True)
    e = jnp.where(c_ok, jnp.exp2(s - m), 0.0)
    pr = e / jnp.maximum(jnp.sum(e, axis=-1, keepdims=True), 1e-30)
    ocmp_ref[...] = _nn(pr.astype(BF16), vc.astype(BF16))
    imp = _nn_exact_lhs01(gsum_ref[...], _nn_exact_rhs01(pr, cover_ref[...]))
    score = _block_scores(imp, qpos)
    n_s = (past + ts + SEL_BLOCK - 1) // SEL_BLOCK
    n_sp = -(-n_s // 8) * 8
    sel_t = _topk_mask(score.T[0:n_sp, :], min(N_SELECT, n_s))
    sel = jnp.concatenate([sel_t, jnp.zeros((imp.shape[1] - n_sp, rows), F32)], axis=0).T
    sel_ref[...] = sel[0:ts, :]


def _cmp_sample(page_table, proj, cache_t, cw, perm, cover, gsum, layer, row0, nb, ts):
    n_pages = page_table.shape[1]
    past = n_pages * PAGE
    rb0 = row0 // ts
    n_out = -(-(past + ts) // CMP_STRIDE) - N_HALF + 1
    n_ch = past // CMP_STRIDE + 2 * PAGE // CMP_STRIDE
    rows = B_HEADS * ts
    n_slots = cover.shape[1]
    const = lambda shape: pl.BlockSpec(shape, lambda b, pt: (0,) * len(shape))
    grid_spec = pltpu.PrefetchScalarGridSpec(
        num_scalar_prefetch=1,
        grid=(nb,),
        in_specs=[pl.BlockSpec((ts, B_HEADS * B_DK), lambda b, pt: (rb0 + b, OFF_BQ // (B_HEADS * B_DK))),
                  pl.BlockSpec((ts, 1024), lambda b, pt: (rb0 + b, OFF_BM // 1024)),
                  pl.BlockSpec(memory_space=pl.ANY),
                  const((CMP_LEN, 128)), const((CMP_LEN, 128)), const((CMP_LEN, 128)),
                  const((8, 256, 256)), const((4, 256, 256)),
                  const((128, 128)), const((128, 128)),
                  const((2 * PAGE, 2 * PAGE)), const((n_out, n_slots)), const((rows, rows))],
        out_specs=[pl.BlockSpec((None, rows, 128), lambda b, pt: (b, 0, 0)),
                   pl.BlockSpec((None, ts, n_slots), lambda b, pt: (b, 0, 0))],
        scratch_shapes=[pltpu.VMEM((2, ROW_B, past), F32),
                        pltpu.VMEM((8, n_ch, 256), BF16), pltpu.VMEM((4, n_ch, 256), BF16),
                        pltpu.VMEM((n_ch, 256), F32), pltpu.VMEM((n_ch, 256), F32),
                        pltpu.VMEM((8, 128), F32), pltpu.SemaphoreType.DMA((2,))],
    )
    return pl.pallas_call(
        functools.partial(_cmp_sample_kernel, layer=layer, n_pages=n_pages, ts=ts, n_out=n_out, past=past),
        grid_spec=grid_spec,
        out_shape=[jax.ShapeDtypeStruct((nb, rows, 128), F32),
                   jax.ShapeDtypeStruct((nb, ts, n_slots), F32)],
        compiler_params=_params("arbitrary"),
        name="cmp_sample",
    )(page_table, proj, proj, cache_t, cw["pek"], cw["pev_lo"], cw["pev_hi"], cw["wk2"], cw["wv4"], cw["w2k"], cw["w2v"],
      perm, cover, gsum)


def _slc_sample_kernel(pt_ref, q_ref, bm_ref, sel_ref, ocmp_ref, win_ref, eneg_ref, et_ref, cache_ref,
                       o_ref, nwin_ref, sbuf_ref, tail_ref, wscr_ref, opad_ref, sem, *, layer, n_pages, ts, chunk, past):
    slot = _gather_step(pt_ref, cache_ref, layer, n_pages, sbuf_ref, sem, lambda: None)
    rows = B_HEADS * ts
    wb = win_ref.shape[0]
    new_slc = bm_ref[:, BM_SLC:BM_SLC + SLOT]
    new_win = bm_ref[:, BM_WIN:BM_WIN + SLOT]
    q = jnp.concatenate([q_ref[:, h * B_DK:(h + 1) * B_DK] for h in range(B_HEADS)], axis=0)
    qb = (q * (SCALE_B * LOG2E)).astype(BF16)
    tq = _mod(lax.broadcasted_iota(jnp.int32, (rows, 1), 0), ts)
    sel = jnp.concatenate([sel_ref[...]] * B_HEADS, axis=0)
    unsel = (1.0 - sel).astype(BF16)

    carry = (jnp.full((rows, 1), NEG, F32), jnp.zeros((rows, 1), F32), jnp.zeros((rows, B_DV), F32))
    for c in range(past // chunk):
        kt = sbuf_ref[slot, 0:B_DK, c * chunk:(c + 1) * chunk].astype(BF16)
        vt = sbuf_ref[slot, B_DK:ROW_B, c * chunk:(c + 1) * chunk].astype(BF16)
        s = _nn(qb, kt) + _nn(unsel[:, 0:128], eneg_ref[c])
        carry = _online_chunk(s, lambda p: _nt(p, vt), carry)
    tail_ref[...] = jnp.zeros_like(tail_ref)
    tail_ref[0:ts, :] = new_slc
    kn = tail_ref[...].astype(BF16)
    tk = lax.broadcasted_iota(jnp.int32, (rows, PAGE), 1)
    in_sel = _nn(sel[:, 128:256].astype(BF16), et_ref[...])
    bias = jnp.where((in_sel > 0.5) & (tk <= tq), 0.0, NEG)
    _, l, acc = _online_chunk(_nt(qb, kn[:, 0:B_DK]) + bias, lambda p: _nn(p, kn[:, B_DK:ROW_B]), carry)
    osel = acc / l

    wscr_ref[...] = jnp.zeros_like(wscr_ref)
    wscr_ref[0:wb, 0:B_DK] = win_ref[:, 0:B_DK]
    wscr_ref[0:wb, B_DK:ROW_B] = win_ref[:, B_DK:ROW_B]
    wscr_ref[wb:wb + ts, :] = new_win
    kw = wscr_ref[...].astype(BF16)
    wpos = past - wb + lax.broadcasted_iota(jnp.int32, (rows, wscr_ref.shape[0]), 1)
    qpos = past + tq
    bias_w = jnp.where((wpos <= qpos) & (wpos >= qpos - WINDOW) & (wpos < past + ts), 0.0, NEG)
    s = _nt(qb, kw[:, 0:B_DK]) + bias_w
    e = jnp.exp2(s - jnp.max(s, axis=-1, keepdims=True))
    owin = _nn(e.astype(BF16), kw[:, B_DK:ROW_B]) / jnp.sum(e, axis=-1, keepdims=True)

    gs = jnp.concatenate([jax.nn.sigmoid(bm_ref[:, BM_G:BM_G + 128])] * B_HEADS, axis=0)
    col = lax.broadcasted_iota(jnp.int32, (rows, 128), 1)
    hd = _div(lax.broadcasted_iota(jnp.int32, (rows, 128), 0), ts)

    def gate(j):
        return jnp.sum(jnp.where(col == 3 * hd + j, gs, 0.0), axis=-1, keepdims=True)

    opad_ref[...] = jnp.zeros_like(opad_ref)
    opad_ref[:, 0:B_DV] = gate(0) * ocmp_ref[:, 0:B_DV] + gate(1) * osel + gate(2) * owin
    o = opad_ref[...]
    for p in range(B_HEADS // 2):
        lo = o[(2 * p) * ts:(2 * p + 1) * ts, :]
        hi = pltpu.roll(o[(2 * p + 1) * ts:(2 * p + 2) * ts, :], 64, 1)
        o_ref[:, p * 128:(p + 1) * 128] = lo + hi

    nwin_ref[0:wb - ts, :] = win_ref[ts:wb, :]
    nwin_ref[wb - ts:wb, :] = new_win[:, 0:ROW_B]


def _slc_sample(page_table, proj, sel, ocmp, state_win, eneg, etail, cache_t, layer, row0, nb, ts, chunk=2048):
    n_pages = page_table.shape[1]
    past = n_pages * PAGE
    rb0 = row0 // ts
    rows = B_HEADS * ts
    wb = state_win.shape[2]
    n_slots = sel.shape[2]
    wrows = -(-(wb + ts) // 128) * 128
    grid_spec = pltpu.PrefetchScalarGridSpec(
        num_scalar_prefetch=1,
        grid=(nb,),
        in_specs=[pl.BlockSpec((ts, B_HEADS * B_DK), lambda b, pt: (rb0 + b, OFF_BQ // (B_HEADS * B_DK))),
                  pl.BlockSpec((ts, 1024), lambda b, pt: (rb0 + b, OFF_BM // 1024)),
                  pl.BlockSpec((None, ts, n_slots), lambda b, pt: (b, 0, 0)),
                  pl.BlockSpec((None, rows, 128), lambda b, pt: (b, 0, 0)),
                  pl.BlockSpec((None, None, wb, ROW_B), lambda b, pt: (b, layer, 0, 0)),
                  pl.BlockSpec(eneg.shape, lambda b, pt: (0, 0, 0)),
                  pl.BlockSpec(etail.shape, lambda b, pt: (0, 0)),
                  pl.BlockSpec(memory_space=pl.ANY)],
        out_specs=[pl.BlockSpec((ts, B_HEADS * B_DV), lambda b, pt: (b, 0)),
                   pl.BlockSpec((None, wb, ROW_B), lambda b, pt: (b, 0, 0))],
        scratch_shapes=[pltpu.VMEM((2, ROW_B, past), F32), pltpu.VMEM((PAGE, SLOT), F32),
                        pltpu.VMEM((wrows, SLOT), F32), pltpu.VMEM((rows, 128), F32),
                        pltpu.SemaphoreType.DMA((2,))],
    )
    return pl.pallas_call(
        functools.partial(_slc_sample_kernel, layer=layer, n_pages=n_pages, ts=ts, chunk=chunk, past=past),
        grid_spec=grid_spec,
        out_shape=[jax.ShapeDtypeStruct((nb * ts, B_HEADS * B_DV), F32),
                   jax.ShapeDtypeStruct((nb, wb, ROW_B), F32)],
        compiler_params=_params("arbitrary"),
        name="slc_sample",
    )(page_table, proj, proj, sel, ocmp, state_win, eneg, etail, cache_t)


def _merge_kernel(ya_ref, ob_ref, olat_ref, bz_ref, cz_ref, gates_ref, wuv_ref, wpa_ref, wpb_ref, wpc_ref, o_ref):
    yb = (ob_ref[...] * _silu(bz_ref[...])).astype(BF16)
    acc = jax.nn.sigmoid(gates_ref[:, 0:D_MODEL]) * _nn(ya_ref[...], wpa_ref[...])
    acc = acc + jax.nn.sigmoid(gates_ref[:, D_MODEL:2 * D_MODEL]) * _nn(yb, wpb_ref[...])
    yc = jnp.concatenate([_nn(olat_ref[h].astype(BF16), wuv_ref[h]) for h in range(C_HEADS)], axis=1)
    yc = (yc * _silu(cz_ref[...])).astype(BF16)
    acc = acc + jax.nn.sigmoid(gates_ref[:, 2 * D_MODEL:3 * D_MODEL]) * _nn(yc, wpc_ref[...])
    o_ref[...] = acc.astype(BF16)


def _merge(ya, ob, olat, proj, wuv, wpa, wpb, wpc, tm=256):
    m = ya.shape[0]
    const = lambda shape: pl.BlockSpec(shape, lambda i: (0,) * len(shape))
    return pl.pallas_call(
        _merge_kernel,
        grid=(m // tm,),
        in_specs=[pl.BlockSpec((tm, A_WIDTH), lambda i: (i, 0)),
                  pl.BlockSpec((tm, B_HEADS * B_DV), lambda i: (i, 0)),
                  pl.BlockSpec((C_HEADS, tm, C_KV_LORA), lambda i: (0, i, 0)),
                  pl.BlockSpec((tm, 1024), lambda i: (i, OFF_BZ // 1024)),
                  pl.BlockSpec((tm, 1024), lambda i: (i, OFF_CZ // 1024)),
                  pl.BlockSpec((tm, 3 * D_MODEL), lambda i: (i, OFF_GATES // (3 * D_MODEL))),
                  const((C_HEADS, C_KV_LORA, 128)), const((A_WIDTH, D_MODEL)),
                  const((B_HEADS * B_DV, D_MODEL)), const((C_HEADS * 128, D_MODEL))],
        out_specs=pl.BlockSpec((tm, D_MODEL), lambda i: (i, 0)),
        out_shape=jax.ShapeDtypeStruct((m, D_MODEL), BF16),
        compiler_params=_params("arbitrary"),
        name="merge",
    )(ya, ob, olat, proj, proj, proj, wuv, wpa, wpb, wpc)


def _outproj_kernel(mg_ref, x_ref, w_ref, g_ref, o_ref):
    out = _nn(mg_ref[...], w_ref[...])
    o_ref[...] = x_ref[...] + _rms(out, g_ref[...])


def _outproj(merged, x, w, g, tm=512):
    m = x.shape[0]
    return pl.pallas_call(
        _outproj_kernel,
        grid=(m // tm,),
        in_specs=[pl.BlockSpec((tm, D_MODEL), lambda i: (i, 0)),
                  pl.BlockSpec((tm, D_MODEL), lambda i: (i, 0)),
                  pl.BlockSpec((D_MODEL, D_MODEL), lambda i: (0, 0)),
                  pl.BlockSpec((1, D_MODEL), lambda i: (0, 0))],
        out_specs=pl.BlockSpec((tm, D_MODEL), lambda i: (i, 0)),
        out_shape=jax.ShapeDtypeStruct((m, D_MODEL), F32),
        compiler_params=_params("arbitrary"),
        name="out_proj",
    )(merged, x, w, g)


def _pack_w_in(w_in):
    (a_uv, a_z, b_q, b_kc, b_vc, b_ks, b_vs, b_kw, b_vw, b_g, b_z, c_dq, c_dkv, c_z, gates) = jnp.split(
        w_in, SPLIT_AT, axis=-1)
    z = lambda n: jnp.zeros(w_in.shape[:-1] + (n,), w_in.dtype)
    kr = c_dkv[..., C_KV_LORA:]
    kr_sw = jnp.concatenate([kr[..., C_ROPE // 2:], kr[..., :C_ROPE // 2]], axis=-1)
    bm = jnp.concatenate([b_kc, b_vc, z(64), b_ks, b_vs, z(64), b_kw, b_vw, z(64), b_g, z(1024 - 768 - 48)], axis=-1)
    cm = jnp.concatenate([c_dq, c_dkv[..., :C_KV_LORA], kr, kr_sw, z(1024 - 768)], axis=-1)
    return jnp.concatenate([gates, a_uv, b_q, a_z, bm, b_z, cm, c_z], axis=-1).astype(BF16)


def _pack_w_uq(w_uq):
    d = w_uq.shape[0]
    w = w_uq.reshape(d, C_Q_LORA, C_HEADS, C_NOPE + C_ROPE)
    nope = w[..., :C_NOPE].reshape(d, C_Q_LORA, C_HEADS * C_NOPE)
    r = w[..., C_NOPE:]
    r_sw = jnp.concatenate([r[..., C_ROPE // 2:], r[..., :C_ROPE // 2]], axis=-1)
    rope = jnp.concatenate([r, r_sw], axis=-1).reshape(d, C_Q_LORA, C_HEADS * 128)
    return jnp.concatenate([nope, rope], axis=-1).astype(BF16)


def _pack_cmp(pe_k, w1_k, w2_k, pe_v, w1_v, w2_v):
    d = pe_k.shape[0]
    wk = jnp.concatenate([w1_k[:, :CMP_STRIDE], w1_k[:, CMP_STRIDE:]], axis=-1)
    zv = jnp.zeros((d, CMP_STRIDE, B_DV, 128 - B_DV), F32)
    wv = jnp.concatenate([w1_v[:, :CMP_STRIDE], zv, w1_v[:, CMP_STRIDE:], zv], axis=-1)
    wk2 = wk.reshape(d, 2, 8, B_DK, 256).transpose(0, 2, 1, 3, 4).reshape(d, 8, 2 * B_DK, 256)
    wv4 = wv.reshape(d, 4, 4, B_DV, 256).transpose(0, 2, 1, 3, 4).reshape(d, 4, 4 * B_DV, 256)
    pad_hi = ((0, 0), (0, 0), (0, 128 - B_DV))
    pad_lo = ((0, 0), (0, 0), (128 - B_DV, 0))
    return dict(pek=pe_k, pev=jnp.pad(pe_v, pad_hi), pev_lo=jnp.pad(pe_v, pad_hi), pev_hi=jnp.pad(pe_v, pad_lo),
                wk=wk.astype(BF16), wv=jnp.pad(wv, ((0, 0), (0, 0), (0, 128 - B_DV), (0, 0))).astype(BF16),
                wk2=wk2.astype(BF16), wv4=wv4.astype(BF16), w2k=w2_k.astype(BF16),
                w2v=jnp.pad(w2_v, ((0, 0), (0, 128 - B_DV), (0, 128 - B_DV))).astype(BF16))


def _rope_table(pos):
    half = C_ROPE // 2
    inv = ROPE_THETA ** (-jnp.arange(half, dtype=F32) / half)
    ang = pos.astype(F32)[:, None] * inv
    cos, sin = jnp.cos(ang), jnp.sin(ang)
    return jnp.concatenate([cos, cos, -sin, sin], axis=-1)


def _cover(n_c, n_c_pad, n_s, n_s_pad):
    ci = np.arange(n_c)[:, None]
    sj = np.arange(n_s)[None, :]
    cov = ((ci * CMP_STRIDE < (sj + 1) * SEL_BLOCK) & (ci * CMP_STRIDE + CMP_LEN > sj * SEL_BLOCK)).astype(np.float32)
    out = np.zeros((n_c_pad, n_s_pad), np.float32)
    out[:n_c, :n_s] = cov
    return jnp.asarray(out, BF16)


def _expand(n_slots, key0, n_keys, slot0=0):
    j = np.arange(n_slots)[:, None] + slot0
    k = (np.arange(n_keys)[None, :] + key0) // SEL_BLOCK
    return (j == k).astype(np.float32)


def kernel(x_prompt, x_sample, cache_mla, cache_nsa_cmp, cache_nsa_slc, state_nsa_win, page_table, pre_norm_g, w_in, a_ln_g, a_ln_b, a_ws, a_bs, b_cmp_pe_k, b_cmp_w1_k, b_cmp_w2_k, b_cmp_pe_v, b_cmp_w1_v, b_cmp_w2_v, c_q_norm_g, c_kv_norm_g, c_w_uq, c_w_uk, c_w_uv, w_proj_a, w_proj_b, w_proj_c, w_out, post_norm_g):
    nbp, tp, _ = x_prompt.shape
    nbs, ts, _ = x_sample.shape
    n_pages = page_table.shape[1]
    past = n_pages * PAGE
    mp = nbp * tp
    ms = nbs * ts
    depth = w_in.shape[0]

    w_in_p = _pack_w_in(w_in)
    w_uq_p = _pack_w_uq(c_w_uq)
    w_ukt = jnp.transpose(c_w_uk, (0, 2, 3, 1)).astype(BF16)
    w_uv = jnp.transpose(c_w_uv, (0, 2, 1, 3)).astype(BF16)
    cw = _pack_cmp(b_cmp_pe_k, b_cmp_w1_k, b_cmp_w2_k, b_cmp_pe_v, b_cmp_w1_v, b_cmp_w2_v)
    wpa, wpb, wpc, wo = (w.astype(BF16) for w in (w_proj_a, w_proj_b, w_proj_c, w_out))
    tab = jnp.concatenate([jnp.tile(_rope_table(jnp.arange(tp)), (nbp, 1)),
                           jnp.tile(_rope_table(past + jnp.arange(ts)), (nbs, 1))], axis=0)
    bs_p = jnp.repeat(jnp.transpose(a_bs, (0, 2, 1)), A_WIDTH // A_GROUPS, axis=-1)
    reps = A_CHUNK // ts
    ws_s = jnp.tile(a_ws[:, :, :ts, :ts], (1, 1, reps, reps))
    bs_s = jnp.tile(bs_p[:, :ts], (1, reps, 1))

    ncp = tp // CMP_STRIDE
    cover_p = _cover(ncp - N_HALF + 1, ncp, tp // SEL_BLOCK, 128)
    eneg_p = jnp.asarray(_expand(128, 0, tp).T * NEG, BF16)
    gsel = np.zeros((128, 3 * (B_HEADS // 2), 128), np.float32)
    for p in range(B_HEADS // 2):
        for j in range(3):
            gsel[3 * (2 * p) + j, 3 * p + j, :B_DV] = 1.0
            gsel[3 * (2 * p + 1) + j, 3 * p + j, B_DV:] = 1.0
    gsel = jnp.asarray(gsel.reshape(128, -1), BF16)
    ncs = -(-(past + ts) // CMP_STRIDE) - N_HALF + 1
    nss = -(-(past + ts) // SEL_BLOCK)
    cover_s = _cover(ncs, ncs, nss, 256)
    chunk = 2048
    eneg_s = jnp.asarray(np.stack([_expand(128, c * chunk, chunk) for c in range(past // chunk)]) * NEG, BF16)
    etail_s = jnp.asarray(_expand(128, past, PAGE, slot0=128), BF16)
    rr = np.arange(B_HEADS * ts)
    gsum = jnp.asarray((rr[:, None] % ts == rr[None, :] % ts).astype(np.float32), BF16)
    cache_mla_t = jnp.swapaxes(cache_mla, 2, 3)
    cache_slc_t = jnp.swapaxes(cache_nsa_slc, 2, 3)
    cache_cmp_t = jnp.swapaxes(cache_nsa_cmp, 2, 3)
    grp = 2 * PAGE
    pos = np.arange(grp)
    perm = np.zeros((grp, grp), np.float32)
    perm[(pos % CMP_STRIDE) * (grp // CMP_STRIDE) + pos // CMP_STRIDE, pos] = 1.0
    perm = jnp.asarray(perm, BF16)

    x = jnp.concatenate([x_prompt.reshape(mp, D_MODEL), x_sample.reshape(ms, D_MODEL)], axis=0)
    outs = {k: [] for k in ("mla_p", "mla_s", "cmp_p", "cmp_s", "slc_p", "slc_s", "win_p", "win_s", "av_s")}
    for l in range(depth):
        cwl = {k: v[l] for k, v in cw.items()}
        proj = _inproj(x, pre_norm_g[l][None], w_in_p[l])
        ya_p = _mixa(proj, 0, mp, a_ln_g[l][None], a_ln_b[l][None], a_ws[l], bs_p[l], sample=False)[0]
        ya_s, vn_s = _mixa(proj, mp, ms, a_ln_g[l][None], a_ln_b[l][None], ws_s[l], bs_s[l], sample=True)
        qcat, ckr = _mlaprep(proj, tab, c_q_norm_g[l][None], c_kv_norm_g[l][None], w_uq_p[l], w_ukt[l])
        kc_p, vc_p = _cmp_prompt(proj, nbp, tp, cwl)
        ob_p = _nsa_prompt(proj, kc_p, vc_p, cover_p, eneg_p, gsel, nbp, tp)
        olat_p = _mla_prompt(qcat, ckr, nbp, tp)
        olat_s = _mla_sample(page_table, qcat, ckr, cache_mla_t, l, mp, nbs, ts, chunk)
        ocmp_s, sel_s = _cmp_sample(page_table, proj, cache_cmp_t, cwl, perm, cover_s, gsum, l, mp, nbs, ts)
        ob_s, nwin_s = _slc_sample(page_table, proj, sel_s, ocmp_s, state_nsa_win, eneg_s, etail_s,
                                   cache_slc_t, l, mp, nbs, ts, chunk)
        ya = jnp.concatenate([ya_p, ya_s], axis=0)
        ob = jnp.concatenate([ob_p, ob_s], axis=0)
        olat = jnp.concatenate([olat_p, olat_s], axis=1)
        merged = _merge(ya, ob, olat, proj, w_uv[l], wpa[l], wpb[l], wpc[l])
        x = _outproj(merged, x, wo[l], post_norm_g[l][None])

        rows = lambda off: proj[:, OFF_BM + off:OFF_BM + off + ROW_B]
        cmp_r, slc_r, win_r = rows(BM_CMP), rows(BM_SLC), rows(BM_WIN)
        outs["mla_p"].append(ckr[:mp].reshape(nbp, tp, ROW_C))
        outs["mla_s"].append(ckr[mp:].reshape(nbs, ts, ROW_C))
        outs["cmp_p"].append(cmp_r[:mp].reshape(nbp, tp, ROW_B))
        outs["cmp_s"].append(cmp_r[mp:].reshape(nbs, ts, ROW_B))
        outs["slc_p"].append(slc_r[:mp].reshape(nbp, tp, ROW_B))
        outs["slc_s"].append(slc_r[mp:].reshape(nbs, ts, ROW_B))
        outs["win_p"].append(win_r[:mp].reshape(nbp, tp, ROW_B)[:, -min(WINDOW, tp):])
        outs["win_s"].append(nwin_s)
        outs["av_s"].append(vn_s.reshape(nbs, ts, A_WIDTH))

    st = lambda k: jnp.stack(outs[k], axis=1)
    return (x[:mp].reshape(nbp, tp, D_MODEL), x[mp:].reshape(nbs, ts, D_MODEL),
            st("mla_p"), st("mla_s"), st("cmp_p"), st("cmp_s"), st("slc_p"), st("slc_s"),
            st("win_p"), st("win_s"), st("av_s"))
```

```python
import functools

import numpy as np
import jax
import jax.numpy as jnp
from jax import lax
from jax.experimental import pallas as pl
from jax.experimental.pallas import tpu as pltpu

F32 = jnp.float32
BF16 = jnp.bfloat16

D_MODEL = 2048
PAGE = 128
A_WIDTH = 1024
A_GROUPS = 8
A_CHUNK = 128
B_HEADS = 16
B_DK = 128
B_DV = 64
CMP_LEN = 32
CMP_STRIDE = 16
SEL_BLOCK = 64
N_SELECT = 16
WINDOW = 512
FORCE_BONUS = 1e4
C_HEADS = 8
C_Q_LORA = 512
C_KV_LORA = 128
C_NOPE = 128
C_ROPE = 64
ROPE_THETA = 10000.0
Q_BLOCK = 128
NORM_EPS = 1e-6
ROW_B = B_DK + B_DV
ROW_C = C_KV_LORA + C_ROPE
SLOT = 256
N_HALF = CMP_LEN // CMP_STRIDE

IN_WIDTHS = (2 * A_WIDTH, A_WIDTH, B_HEADS * B_DK, B_DK, B_DV, B_DK, B_DV, B_DK, B_DV, 3 * B_HEADS,
             B_HEADS * B_DV, C_Q_LORA, C_KV_LORA + C_ROPE, C_HEADS * 128, 3 * D_MODEL)
SPLIT_AT = tuple(int(s) for s in np.cumsum(IN_WIDTHS)[:-1])

OFF_GATES, OFF_AUV, OFF_BQ, OFF_AZ, OFF_BM, OFF_BZ, OFF_CM, OFF_CZ = (
    0, 6144, 8192, 10240, 11264, 12288, 13312, 14336)
PW = 15360
BM_CMP, BM_SLC, BM_WIN, BM_G = 0, 256, 512, 768
CM_DQ, CM_KV, CM_KR = 0, 512, 640

SCALE_B = B_DK ** -0.5
SCALE_C = (C_NOPE + C_ROPE) ** -0.5
NEG = -1e30
LOG2E = 1.4426950408889634
VMEM_LIMIT = 56 * 1024 * 1024


def _nt(a, b):
    return lax.dot_general(a, b, (((1,), (1,)), ((), ())), preferred_element_type=F32)


def _nn(a, b):
    return jnp.dot(a, b, preferred_element_type=F32)


def _split3(x):
    h = x.astype(BF16)
    r = x - h.astype(F32)
    m = r.astype(BF16)
    lo = (r - m.astype(F32)).astype(BF16)
    return h, m, lo


def _nn_exact_rhs01(x, w01):
    h, m, lo = _split3(x)
    return _nn(h, w01) + _nn(m, w01) + _nn(lo, w01)


def _nn_exact_lhs01(w01, x):
    h, m, lo = _split3(x)
    return _nn(w01, h) + _nn(w01, m) + _nn(w01, lo)


def _log2(n):
    assert n & (n - 1) == 0, n
    return n.bit_length() - 1


def _div(x, n):
    return jnp.right_shift(x, _log2(n))


def _mod(x, n):
    return jnp.bitwise_and(x, n - 1)


def _params(*sem):
    return pltpu.CompilerParams(dimension_semantics=sem, vmem_limit_bytes=VMEM_LIMIT)


def _rms(x, g):
    return x * lax.rsqrt(jnp.mean(x * x, axis=-1, keepdims=True) + NORM_EPS) * g


def _silu(x):
    return x * jax.nn.sigmoid(x)


def _inproj_kernel(x_ref, g_ref, w_ref, o_ref, h_ref):
    @pl.when(pl.program_id(1) == 0)
    def _():
        h_ref[...] = _rms(x_ref[...], g_ref[...]).astype(BF16)

    o_ref[...] = _nn(h_ref[...], w_ref[...])


def _inproj(x, g, w, tm=1024, tn=1024):
    m = x.shape[0]
    return pl.pallas_call(
        _inproj_kernel,
        grid=(m // tm, PW // tn),
        in_specs=[pl.BlockSpec((tm, D_MODEL), lambda i, j: (i, 0)),
                  pl.BlockSpec((1, D_MODEL), lambda i, j: (0, 0)),
                  pl.BlockSpec((D_MODEL, tn), lambda i, j: (0, j))],
        out_specs=pl.BlockSpec((tm, tn), lambda i, j: (i, j)),
        out_shape=jax.ShapeDtypeStruct((m, PW), F32),
        scratch_shapes=[pltpu.VMEM((tm, D_MODEL), BF16)],
        compiler_params=_params("arbitrary", "arbitrary"),
        name="in_proj",
    )(x, g, w)


def _mixa_kernel(auv_ref, az_ref, lng_ref, lnb_ref, ws_ref, bs_ref, y_ref, *vn_out, sample):
    auv = jax.nn.gelu(auv_ref[...])
    u = auv[:, :A_WIDTH]
    v = auv[:, A_WIDTH:]
    mu = jnp.mean(v, axis=-1, keepdims=True)
    var = jnp.mean(jnp.square(v - mu), axis=-1, keepdims=True)
    vn = (v - mu) * lax.rsqrt(var + NORM_EPS) * lng_ref[...] + lnb_ref[...]
    if vn_out:
        vn_out[0][...] = vn
    r = lax.broadcasted_iota(jnp.int32, (A_CHUNK, A_CHUNK), 0)
    c = lax.broadcasted_iota(jnp.int32, (A_CHUNK, A_CHUNK), 1)
    mask = c <= r
    if sample:
        mask = mask & (_div(r, 8) == _div(c, 8))
    vb = vn.astype(BF16)
    gw = A_WIDTH // A_GROUPS
    for g in range(A_GROUPS):
        w = jnp.where(mask, ws_ref[g], 0.0).astype(BF16)
        s = _nn(w, vb[:, g * gw:(g + 1) * gw]) + bs_ref[:, g * gw:(g + 1) * gw]
        y = u[:, g * gw:(g + 1) * gw] * s * _silu(az_ref[:, g * gw:(g + 1) * gw])
        y_ref[:, g * gw:(g + 1) * gw] = y.astype(BF16)


def _mixa(proj, row0, nrows, lng, lnb, ws, bs, sample):
    rb0 = row0 // A_CHUNK
    nb = nrows // A_CHUNK
    out_shape = [jax.ShapeDtypeStruct((nrows, A_WIDTH), BF16)]
    out_specs = [pl.BlockSpec((A_CHUNK, A_WIDTH), lambda i: (i, 0))]
    if sample:
        out_shape.append(jax.ShapeDtypeStruct((nrows, A_WIDTH), F32))
        out_specs.append(pl.BlockSpec((A_CHUNK, A_WIDTH), lambda i: (i, 0)))
    return pl.pallas_call(
        functools.partial(_mixa_kernel, sample=sample),
        grid=(nb,),
        in_specs=[pl.BlockSpec((A_CHUNK, 2 * A_WIDTH), lambda i: (rb0 + i, OFF_AUV // (2 * A_WIDTH))),
                  pl.BlockSpec((A_CHUNK, A_WIDTH), lambda i: (rb0 + i, OFF_AZ // A_WIDTH)),
                  pl.BlockSpec((1, A_WIDTH), lambda i: (0, 0)),
                  pl.BlockSpec((1, A_WIDTH), lambda i: (0, 0)),
                  pl.BlockSpec((A_GROUPS, A_CHUNK, A_CHUNK), lambda i: (0, 0, 0)),
                  pl.BlockSpec((A_CHUNK, A_WIDTH), lambda i: (0, 0))],
        out_specs=out_specs,
        out_shape=out_shape,
        compiler_params=_params("arbitrary"),
        name="mix_a_sample" if sample else "mix_a_prompt",
    )(proj, proj, lng, lnb, ws, bs)


def _mlaprep_kernel(cm_ref, tab_ref, qg_ref, kvg_ref, wuq_ref, wukt_ref, qcat_ref, ckr_ref):
    tab = tab_ref[...]
    h = _rms(cm_ref[:, CM_DQ:CM_DQ + C_Q_LORA], qg_ref[...]).astype(BF16)
    q = _nn(h, wuq_ref[...])
    lo = lax.broadcasted_iota(jnp.int32, tab.shape, 1) < C_ROPE
    for hh in range(C_HEADS):
        qn = q[:, hh * C_NOPE:(hh + 1) * C_NOPE].astype(BF16)
        qcat_ref[hh, :, 0:C_KV_LORA] = _nn(qn, wukt_ref[hh])
        pr = q[:, C_HEADS * C_NOPE + hh * 128:C_HEADS * C_NOPE + (hh + 1) * 128] * tab
        ro = pr + pltpu.roll(pr, 64, 1)
        qcat_ref[hh, :, C_KV_LORA:SLOT] = jnp.where(lo, ro, 0.0)
    ckr_ref[:, 0:C_KV_LORA] = _rms(cm_ref[:, CM_KV:CM_KV + C_KV_LORA], kvg_ref[...])
    prk = cm_ref[:, CM_KR:CM_KR + 128] * tab
    rok = prk + pltpu.roll(prk, 64, 1)
    ckr_ref[:, C_KV_LORA:ROW_C] = rok[:, 0:C_ROPE]


def _mlaprep(proj, tab, qg, kvg, wuq, wukt, tm=256):
    m = proj.shape[0]
    return pl.pallas_call(
        _mlaprep_kernel,
        grid=(m // tm,),
        in_specs=[pl.BlockSpec((tm, 1024), lambda i: (i, OFF_CM // 1024)),
                  pl.BlockSpec((tm, 128), lambda i: (i, 0)),
                  pl.BlockSpec((1, C_Q_LORA), lambda i: (0, 0)),
                  pl.BlockSpec((1, C_KV_LORA), lambda i: (0, 0)),
                  pl.BlockSpec((C_Q_LORA, 2048), lambda i: (0, 0)),
                  pl.BlockSpec((C_HEADS, C_NOPE, C_KV_LORA), lambda i: (0, 0, 0))],
        out_specs=[pl.BlockSpec((C_HEADS, tm, SLOT), lambda i: (0, i, 0)),
                   pl.BlockSpec((tm, ROW_C), lambda i: (i, 0))],
        out_shape=[jax.ShapeDtypeStruct((C_HEADS, m, SLOT), F32),
                   jax.ShapeDtypeStruct((m, ROW_C), F32)],
        compiler_params=_params("arbitrary"),
        name="mla_prep",
    )(proj, tab, qg, kvg, wuq, wukt)


def _attend_many(chains):
    ss = [_nt(q, k) + b for q, k, b, _ in chains]
    es = [jnp.exp2(s - jnp.max(s, axis=-1, keepdims=True)) for s in ss]
    return [_nn(e.astype(BF16), c[3]) / jnp.sum(e, axis=-1, keepdims=True) for e, c in zip(es, chains)]


def _causal_widths(t, step):
    return [w for w in range(step, t + step, step)]


def _mla_prompt_kernel(q_ref, k_ref, o_ref, kbf_ref, qs_ref, bias_ref, *, step):
    i = pl.program_id(1)
    t = k_ref.shape[0]

    @pl.when(i == 0)
    def _():
        kbf_ref[...] = jnp.zeros_like(kbf_ref)
        kbf_ref[:, 0:C_KV_LORA] = k_ref[:, 0:C_KV_LORA].astype(BF16)
        kbf_ref[:, C_KV_LORA:ROW_C] = k_ref[:, C_KV_LORA:ROW_C].astype(BF16)

    for h in range(C_HEADS):
        qs_ref[h] = (q_ref[h] * (SCALE_C * LOG2E)).astype(BF16)
    qpos = i * Q_BLOCK + lax.broadcasted_iota(jnp.int32, (Q_BLOCK, t), 0)
    kpos = lax.broadcasted_iota(jnp.int32, (Q_BLOCK, t), 1)
    bias_ref[...] = jnp.where(kpos <= qpos, 0.0, NEG)

    for w in _causal_widths(t, step):
        @pl.when((i * Q_BLOCK) // step == w // step - 1)
        def _(w=w):
            def heads(g, carry):
                hs = [g * 4 + j for j in range(4)]
                outs = _attend_many([(qs_ref[h], kbf_ref[0:w, :], bias_ref[:, 0:w], kbf_ref[0:w, 0:C_KV_LORA])
                                     for h in hs])
                for h, o in zip(hs, outs):
                    o_ref[h] = o
                return carry
            lax.fori_loop(0, C_HEADS // 4, heads, 0)


def _mla_prompt(qcat, ckr, nb, t, step=512):
    nq = t // Q_BLOCK
    return pl.pallas_call(
        functools.partial(_mla_prompt_kernel, step=step),
        grid=(nb, nq),
        in_specs=[pl.BlockSpec((C_HEADS, Q_BLOCK, SLOT), lambda b, i: (0, b * nq + i, 0)),
                  pl.BlockSpec((t, ROW_C), lambda b, i: (b, 0))],
        out_specs=pl.BlockSpec((C_HEADS, Q_BLOCK, C_KV_LORA), lambda b, i: (0, b * nq + i, 0)),
        out_shape=jax.ShapeDtypeStruct((C_HEADS, nb * t, C_KV_LORA), F32),
        scratch_shapes=[pltpu.VMEM((t, SLOT), BF16), pltpu.VMEM((C_HEADS, Q_BLOCK, SLOT), BF16),
                        pltpu.VMEM((Q_BLOCK, t), F32)],
        compiler_params=_params("arbitrary", "arbitrary"),
        name="mla_prompt",
    )(qcat, ckr)


def _pe_bias(pek_ref, pev_ref, wk_ref, wv_ref):
    pk = pek_ref[...].astype(BF16)
    pv = pev_ref[...].astype(BF16)
    bk = jnp.zeros((1, 128), F32)
    bv = jnp.zeros((1, 128), F32)
    for l in range(CMP_STRIDE):
        fk = _nn(pk, wk_ref[l])
        fv = _nn(pv, wv_ref[l])
        bk = bk + fk[l:l + 1, 0:128] + fk[CMP_STRIDE + l:CMP_STRIDE + l + 1, 128:256]
        bv = bv + fv[l:l + 1, 0:128] + fv[CMP_STRIDE + l:CMP_STRIDE + l + 1, 128:256]
    return bk, bv


def _compress(k_at, v_at, n_ch, n_out, bk, bv, wk_ref, wv_ref, w2k_ref, w2v_ref, acck_ref, accv_ref):
    acck = jnp.zeros((n_ch, 2 * B_DK), F32)
    accv = jnp.zeros((n_ch, 2 * 128), F32)
    for l in range(CMP_STRIDE):
        acck = acck + _nn(k_at(l).astype(BF16), wk_ref[l])
        accv = accv + _nn(v_at(l).astype(BF16), wv_ref[l])
    acck_ref[...] = acck
    accv_ref[...] = accv
    hk = acck_ref[0:n_out, 0:B_DK] + acck_ref[1:n_out + 1, B_DK:2 * B_DK] + bk
    hv = accv_ref[0:n_out, 0:128] + accv_ref[1:n_out + 1, 128:256] + bv
    kc = _nn(_silu(hk).astype(BF16), w2k_ref[...])
    vc = _nn(_silu(hv).astype(BF16), w2v_ref[...])
    return kc, vc


def _topk_mask(sc, nsel):
    nb, r = sc.shape
    idx = lax.broadcasted_iota(jnp.int32, (nb, r), 0).astype(F32)

    def body(_, carry):
        sc, sel = carry
        m = jnp.max(sc, axis=0, keepdims=True)
        ism = (sc == m) & (m > -jnp.inf)
        first = jnp.min(jnp.where(ism, idx, float(nb)), axis=0, keepdims=True)
        pick = idx == first
        return jnp.where(pick, -jnp.inf, sc), jnp.where(pick, 1.0, sel)

    _, sel = lax.fori_loop(0, nsel, body, (sc, jnp.zeros((nb, r), F32)))
    return sel


def _block_scores(imp, qpos):
    jj = lax.broadcasted_iota(jnp.int32, imp.shape, 1)
    cur = _div(qpos, SEL_BLOCK)
    valid = jj * SEL_BLOCK <= qpos
    forced = (jj == 0) | (jj == cur) | (jj == cur - 1)
    return jnp.where(valid, imp + jnp.where(forced, FORCE_BONUS, 0.0), -jnp.inf)


def _cmp_prompt_kernel(bm_ref, pek_ref, pev_ref, wk_ref, wv_ref, w2k_ref, w2v_ref, kc_ref, vc_ref,
                       ks_ref, vs_ref, acck_ref, accv_ref, *, t, n_ch, n_out):
    ks_ref[...] = jnp.zeros_like(ks_ref)
    vs_ref[...] = jnp.zeros_like(vs_ref)
    ks_ref[0:t, :] = bm_ref[:, BM_CMP:BM_CMP + B_DK]
    vs_ref[0:t, :] = bm_ref[:, BM_CMP + B_DK:BM_CMP + SLOT]
    bk, bv = _pe_bias(pek_ref, pev_ref, wk_ref, wv_ref)
    kc, vc = _compress(lambda l: ks_ref[pl.ds(l, n_ch, stride=CMP_STRIDE), :],
                       lambda l: vs_ref[pl.ds(l, n_ch, stride=CMP_STRIDE), :],
                       n_ch, n_out, bk, bv, wk_ref, wv_ref, w2k_ref, w2v_ref, acck_ref, accv_ref)
    kc_ref[...] = kc
    vc_ref[...] = vc


def _cmp_prompt(proj, nb, t, cw):
    n_out = t // CMP_STRIDE
    n_ch = n_out + 8
    rows = n_ch * CMP_STRIDE
    return pl.pallas_call(
        functools.partial(_cmp_prompt_kernel, t=t, n_ch=n_ch, n_out=n_out),
        grid=(nb,),
        in_specs=[pl.BlockSpec((t, 1024), lambda b: (b, OFF_BM // 1024)),
                  pl.BlockSpec((CMP_LEN, 128), lambda b: (0, 0)),
                  pl.BlockSpec((CMP_LEN, 128), lambda b: (0, 0)),
                  pl.BlockSpec((CMP_STRIDE, 128, 256), lambda b: (0, 0, 0)),
                  pl.BlockSpec((CMP_STRIDE, 128, 256), lambda b: (0, 0, 0)),
                  pl.BlockSpec((128, 128), lambda b: (0, 0)),
                  pl.BlockSpec((128, 128), lambda b: (0, 0))],
        out_specs=[pl.BlockSpec((None, n_out, 128), lambda b: (b, 0, 0)),
                   pl.BlockSpec((None, n_out, 128), lambda b: (b, 0, 0))],
        out_shape=[jax.ShapeDtypeStruct((nb, n_out, 128), F32),
                   jax.ShapeDtypeStruct((nb, n_out, 128), F32)],
        scratch_shapes=[pltpu.VMEM((rows, 128), F32), pltpu.VMEM((rows, 128), F32),
                        pltpu.VMEM((n_ch, 256), F32), pltpu.VMEM((n_ch, 256), F32)],
        compiler_params=_params("arbitrary"),
        name="cmp_prompt",
    )(proj, cw["pek"], cw["pev"], cw["wk"], cw["wv"], cw["w2k"], cw["w2v"])


def _nsa_prompt_kernel(q_ref, g_ref, bm_ref, kc_ref, vc_ref, cover_ref, eneg_ref, gsel_ref, o_ref,
                       qa_ref, kaug_ref, slv_ref, wk_ref, wv_ref, kcb_ref, vcb_ref,
                       ocmp_ref, gsp_ref, bias_s_ref, bias_w_ref, opair_ref, *, t, step):
    i = pl.program_id(1)
    n_pair = B_HEADS // 2
    wrows = WINDOW + Q_BLOCK

    @pl.when(i == 0)
    def _():
        kaug_ref[:, 0:B_DK] = bm_ref[:, BM_SLC:BM_SLC + B_DK].astype(BF16)
        kaug_ref[:, B_DK:2 * B_DK] = eneg_ref[...]
        v = bm_ref[:, BM_SLC + B_DK:BM_SLC + SLOT]
        slv_ref[0] = v.astype(BF16)
        slv_ref[1] = pltpu.roll(v, 64, 1).astype(BF16)
        wk_ref[...] = bm_ref[:, BM_WIN:BM_WIN + B_DK].astype(BF16)
        v = bm_ref[:, BM_WIN + B_DK:BM_WIN + SLOT]
        wv_ref[0] = v.astype(BF16)
        wv_ref[1] = pltpu.roll(v, 64, 1).astype(BF16)
        kcb_ref[...] = kc_ref[...].astype(BF16)
        vcb_ref[0] = vc_ref[...].astype(BF16)
        vcb_ref[1] = pltpu.roll(vc_ref[...], 64, 1).astype(BF16)

    for h in range(B_HEADS):
        qa_ref[h, :, 0:B_DK] = (q_ref[:, h * B_DK:(h + 1) * B_DK] * (SCALE_B * LOG2E)).astype(BF16)

    gs = jax.nn.sigmoid(g_ref[:, BM_G:BM_G + 128])
    g_hi = gs.astype(BF16)
    g_lo = (gs - g_hi.astype(F32)).astype(BF16)
    gall = _nn(g_hi, gsel_ref[...]) + _nn(g_lo, gsel_ref[...])
    for c in range(3 * n_pair):
        gsp_ref[c] = gall[:, c * 128:(c + 1) * 128]

    qpos1 = i * Q_BLOCK + lax.broadcasted_iota(jnp.int32, (Q_BLOCK, 1), 0)

    n_c = kcb_ref.shape[0]
    c_end = lax.broadcasted_iota(jnp.int32, (Q_BLOCK, n_c), 1) * CMP_STRIDE + (CMP_LEN - 1)
    c_ok = c_end <= qpos1
    rows = B_HEADS * Q_BLOCK
    s = _nt(qa_ref[:, :, 0:B_DK].reshape(rows, B_DK), kcb_ref[...]).reshape(B_HEADS, Q_BLOCK, n_c)
    s = jnp.where(c_ok[None], s, NEG)
    e = jnp.where(c_ok[None], jnp.exp2(s - jnp.max(s, axis=-1, keepdims=True)), 0.0)
    pr = e / jnp.maximum(jnp.sum(e, axis=-1, keepdims=True), 1e-30)
    psum = jnp.sum(pr, axis=0)
    prb = pr.astype(BF16).reshape(n_pair, 2, Q_BLOCK, n_c)
    ocmp_ref[...] = (_nn(prb[:, 0].reshape(n_pair * Q_BLOCK, n_c), vcb_ref[0])
                     + _nn(prb[:, 1].reshape(n_pair * Q_BLOCK, n_c), vcb_ref[1])).reshape(n_pair, Q_BLOCK, 128)

    imp = _nn_exact_rhs01(psum, cover_ref[...])
    score = _block_scores(imp, qpos1)
    n_s = t // SEL_BLOCK
    sel_t = _topk_mask(score.T[0:n_s, :], min(N_SELECT, n_s))
    sel = jnp.concatenate([sel_t, jnp.zeros((128 - n_s, Q_BLOCK), F32)], axis=0).T
    unsel = (1.0 - sel).astype(BF16)
    for h in range(B_HEADS):
        qa_ref[h, :, B_DK:2 * B_DK] = unsel

    kpos = lax.broadcasted_iota(jnp.int32, (Q_BLOCK, t), 1)
    bias_s_ref[...] = jnp.where(kpos <= qpos1, 0.0, NEG)
    ws = pl.multiple_of(jnp.maximum(i * Q_BLOCK - WINDOW, 0), Q_BLOCK)
    wpos = ws + lax.broadcasted_iota(jnp.int32, (Q_BLOCK, wrows), 1)
    bias_w_ref[...] = jnp.where((wpos <= qpos1) & (wpos >= qpos1 - WINDOW), 0.0, NEG)

    for w in _causal_widths(t, step):
        @pl.when((i * Q_BLOCK) // step == w // step - 1)
        def _(w=w):
            def pairs(g, carry):
                ps = [2 * g, 2 * g + 1]
                chains = []
                for p in ps:
                    for hh in range(2):
                        h = 2 * p + hh
                        chains.append((qa_ref[h], kaug_ref[0:w, :], bias_s_ref[:, 0:w], slv_ref[hh, 0:w, :]))
                        chains.append((qa_ref[h, :, 0:B_DK], wk_ref[pl.ds(ws, wrows), :], bias_w_ref[...],
                                       wv_ref[hh, pl.ds(ws, wrows), :]))
                outs = _attend_many(chains)
                for k, p in enumerate(ps):
                    se, we, so, wo = outs[4 * k:4 * k + 4]
                    opair_ref[p] = (gsp_ref[3 * p] * ocmp_ref[p] + gsp_ref[3 * p + 1] * (se + so)
                                    + gsp_ref[3 * p + 2] * (we + wo))
                return carry
            lax.fori_loop(0, n_pair // 2, pairs, 0)

    for p in range(n_pair):
        o_ref[:, p * 128:(p + 1) * 128] = opair_ref[p]


def _nsa_prompt(proj, kc, vc, cover, eneg, gsel, nb, t, step=512):
    nq = t // Q_BLOCK
    n_c = kc.shape[1]
    n_pair = B_HEADS // 2
    return pl.pallas_call(
        functools.partial(_nsa_prompt_kernel, t=t, step=step),
        grid=(nb, nq),
        in_specs=[pl.BlockSpec((Q_BLOCK, B_HEADS * B_DK), lambda b, i: (b * nq + i, OFF_BQ // (B_HEADS * B_DK))),
                  pl.BlockSpec((Q_BLOCK, 1024), lambda b, i: (b * nq + i, OFF_BM // 1024)),
                  pl.BlockSpec((t, 1024), lambda b, i: (b, OFF_BM // 1024)),
                  pl.BlockSpec((None, n_c, 128), lambda b, i: (b, 0, 0)),
                  pl.BlockSpec((None, n_c, 128), lambda b, i: (b, 0, 0)),
                  pl.BlockSpec((n_c, 128), lambda b, i: (0, 0)),
                  pl.BlockSpec((t, 128), lambda b, i: (0, 0)),
                  pl.BlockSpec((128, 3 * n_pair * 128), lambda b, i: (0, 0))],
        out_specs=pl.BlockSpec((Q_BLOCK, B_HEADS * B_DV), lambda b, i: (b * nq + i, 0)),
        out_shape=jax.ShapeDtypeStruct((nb * t, B_HEADS * B_DV), F32),
        scratch_shapes=[pltpu.VMEM((B_HEADS, Q_BLOCK, 2 * B_DK), BF16),
                        pltpu.VMEM((t, 2 * B_DK), BF16), pltpu.VMEM((2, t, 128), BF16),
                        pltpu.VMEM((t, B_DK), BF16), pltpu.VMEM((2, t, 128), BF16),
                        pltpu.VMEM((n_c, B_DK), BF16), pltpu.VMEM((2, n_c, 128), BF16),
                        pltpu.VMEM((n_pair, Q_BLOCK, 128), F32),
                        pltpu.VMEM((3 * n_pair, Q_BLOCK, 128), F32),
                        pltpu.VMEM((Q_BLOCK, t), F32),
                        pltpu.VMEM((Q_BLOCK, WINDOW + Q_BLOCK), F32),
                        pltpu.VMEM((n_pair, Q_BLOCK, 128), F32)],
        compiler_params=_params("arbitrary", "arbitrary"),
        name="nsa_prompt",
    )(proj, proj, proj, kc, vc, cover, eneg, gsel)


def _page_copy(pt_ref, cache_ref, layer, bb, slot, buf, sem):
    def at(j):
        return pltpu.make_async_copy(cache_ref.at[pt_ref[bb, j], layer], buf.at[slot, :, pl.ds(j * PAGE, PAGE)],
                                     sem.at[slot])
    return at


def _gather_step(pt_ref, cache_ref, layer, n_pages, buf, sem, init):
    b = pl.program_id(0)
    nb = pl.num_programs(0)
    slot = lax.rem(b, 2)

    def start(bb, sl):
        at = _page_copy(pt_ref, cache_ref, layer, bb, sl, buf, sem)

        def body(j, c):
            at(j).start()
            return c
        lax.fori_loop(0, n_pages, body, 0, unroll=8)

    @pl.when(b == 0)
    def _():
        init()
        start(0, 0)

    @pl.when(b + 1 < nb)
    def _():
        start(b + 1, 1 - slot)

    at = _page_copy(pt_ref, cache_ref, layer, b, slot, buf, sem)

    def wbody(j, c):
        at(j).wait()
        return c
    lax.fori_loop(0, n_pages, wbody, 0, unroll=8)
    return slot


def _online_chunk(s, pv, carry):
    m, l, acc = carry
    m_new = jnp.maximum(m, jnp.max(s, axis=-1, keepdims=True))
    a = jnp.exp2(m - m_new)
    p = jnp.exp2(s - m_new)
    return m_new, a * l + jnp.sum(p, axis=-1, keepdims=True), a * acc + pv(p.astype(BF16))


def _mla_sample_kernel(pt_ref, q_ref, new_ref, cache_ref, o_ref, kbuf_ref, tail_ref, sem, *, layer, n_pages, ts, chunk):
    slot = _gather_step(pt_ref, cache_ref, layer, n_pages, kbuf_ref, sem, lambda: None)
    rows = C_HEADS * ts
    q = q_ref[...].reshape(rows, SLOT)[:, 0:ROW_C]
    qb = (q * (SCALE_C * LOG2E)).astype(BF16)
    tail_ref[...] = jnp.zeros_like(tail_ref)
    tail_ref[0:ts, :] = new_ref[...]

    carry = (jnp.full((rows, 1), NEG, F32), jnp.zeros((rows, 1), F32), jnp.zeros((rows, C_KV_LORA), F32))
    for c in range(n_pages * PAGE // chunk):
        kt = kbuf_ref[slot, :, c * chunk:(c + 1) * chunk].astype(BF16)
        carry = _online_chunk(_nn(qb, kt), lambda p: _nt(p, kt[0:C_KV_LORA, :]), carry)
    tq = _mod(lax.broadcasted_iota(jnp.int32, (rows, PAGE), 0), ts)
    tk = lax.broadcasted_iota(jnp.int32, (rows, PAGE), 1)
    kn = tail_ref[...].astype(BF16)
    _, l, acc = _online_chunk(_nt(qb, kn) + jnp.where(tk <= tq, 0.0, NEG), lambda p: _nn(p, kn[:, 0:C_KV_LORA]), carry)
    o = acc / l
    for h in range(C_HEADS):
        o_ref[h] = o[h * ts:(h + 1) * ts, :]


def _mla_sample(page_table, qcat, ckr, cache_t, layer, row0, nb, ts, chunk=2048):
    n_pages = page_table.shape[1]
    rb0 = row0 // ts
    grid_spec = pltpu.PrefetchScalarGridSpec(
        num_scalar_prefetch=1,
        grid=(nb,),
        in_specs=[pl.BlockSpec((C_HEADS, ts, SLOT), lambda b, pt: (0, rb0 + b, 0)),
                  pl.BlockSpec((ts, ROW_C), lambda b, pt: (rb0 + b, 0)),
                  pl.BlockSpec(memory_space=pl.ANY)],
        out_specs=pl.BlockSpec((C_HEADS, ts, C_KV_LORA), lambda b, pt: (0, b, 0)),
        scratch_shapes=[pltpu.VMEM((2, ROW_C, n_pages * PAGE), F32), pltpu.VMEM((PAGE, ROW_C), F32),
                        pltpu.SemaphoreType.DMA((2,))],
    )
    return pl.pallas_call(
        functools.partial(_mla_sample_kernel, layer=layer, n_pages=n_pages, ts=ts, chunk=chunk),
        grid_spec=grid_spec,
        out_shape=jax.ShapeDtypeStruct((C_HEADS, nb * ts, C_KV_LORA), F32),
        compiler_params=_params("arbitrary"),
        name="mla_sample",
    )(page_table, qcat, ckr, cache_t)


def _cmp_sample_kernel(pt_ref, q_ref, bm_ref, cache_ref, pek_ref, pevl_ref, pevh_ref, wk_ref, wv_ref, w2k_ref, w2v_ref,
                       perm_ref, cover_ref, gsum_ref, ocmp_ref, sel_ref, cbuf_ref, xk_ref, xv_ref, acck_ref, accv_ref,
                       pb_ref, sem, *, layer, n_pages, ts, n_out, past):
    grp = 2 * PAGE
    cpg = grp // CMP_STRIDE
    n_grp = past // grp

    def phase_weights(l):
        wk = wk_ref[l % 8, (l // 8) * 128:(l // 8 + 1) * 128, :]
        wv = wv_ref[l % 4, (l // 8) * 128:(l // 8 + 1) * 128, :]
        return wk, wv

    def init():
        pk = pek_ref[...].astype(BF16)
        pvl = pevl_ref[...].astype(BF16)
        pvh = pevh_ref[...].astype(BF16)
        bk = jnp.zeros((1, 128), F32)
        bv = jnp.zeros((1, 128), F32)
        for l in range(CMP_STRIDE):
            wk, wv = phase_weights(l)
            fk = _nn(pk, wk)
            fv = _nn(pvl if (l // 4) % 2 == 0 else pvh, wv)
            bk = bk + fk[l:l + 1, 0:128] + fk[CMP_STRIDE + l:CMP_STRIDE + l + 1, 128:256]
            bv = bv + fv[l:l + 1, 0:128] + fv[CMP_STRIDE + l:CMP_STRIDE + l + 1, 128:256]
        pb_ref[0:1, :] = bk
        pb_ref[1:2, :] = bv

    slot = _gather_step(pt_ref, cache_ref, layer, n_pages, cbuf_ref, sem, init)
    rows = B_HEADS * ts

    perm = perm_ref[...]
    for g in range(n_grp):
        xt = cbuf_ref[slot, :, g * grp:(g + 1) * grp].astype(BF16)
        y = _nt(perm, jnp.concatenate([xt, xt[B_DK:ROW_B]], axis=0)).astype(BF16)
        for l in range(CMP_STRIDE):
            yl = y[l * cpg:(l + 1) * cpg]
            xk_ref[l % 8, g * cpg:(g + 1) * cpg, (l // 8) * 128:(l // 8 + 1) * 128] = yl[:, 0:B_DK]
            c0 = ((l // 4) % 2) * B_DV
            xv_ref[l % 4, g * cpg:(g + 1) * cpg, (l // 4) * B_DV:(l // 4 + 1) * B_DV] = yl[:, B_DK + c0:B_DK + c0 + B_DV]
    new_k = bm_ref[:, BM_CMP:BM_CMP + B_DK]
    new_v = bm_ref[:, BM_CMP + B_DK:BM_CMP + SLOT]
    new_vh = pltpu.roll(new_v, 64, 1)
    first = lax.broadcasted_iota(jnp.int32, (cpg, 128), 0) == 0
    c0 = n_grp * cpg
    for l in range(CMP_STRIDE):
        if l < ts:
            tk = jnp.where(first, jnp.broadcast_to(new_k[l:l + 1], (cpg, 128)), 0.0)
            nv = new_v if (l // 4) % 2 == 0 else new_vh
            tv = jnp.where(first, jnp.broadcast_to(nv[l:l + 1], (cpg, 128)), 0.0)
        else:
            tk = tv = jnp.zeros((cpg, 128), F32)
        xk_ref[l % 8, c0:c0 + cpg, (l // 8) * 128:(l // 8 + 1) * 128] = tk.astype(BF16)
        c1 = ((l // 4) % 2) * B_DV
        xv_ref[l % 4, c0:c0 + cpg, (l // 4) * B_DV:(l // 4 + 1) * B_DV] = tv[:, c1:c1 + B_DV].astype(BF16)

    acck = _nn(xk_ref[0], wk_ref[0])
    for j in range(1, 8):
        acck = acck + _nn(xk_ref[j], wk_ref[j])
    accv = _nn(xv_ref[0], wv_ref[0])
    for j in range(1, 4):
        accv = accv + _nn(xv_ref[j], wv_ref[j])
    acck_ref[...] = acck
    accv_ref[...] = accv
    hk = acck_ref[0:n_out, 0:B_DK] + acck_ref[1:n_out + 1, B_DK:2 * B_DK] + pb_ref[0:1, :]
    hv = accv_ref[0:n_out, 0:128] + accv_ref[1:n_out + 1, 128:256] + pb_ref[1:2, :]
    kc = _nn(_silu(hk).astype(BF16), w2k_ref[...])
    vc = _nn(_silu(hv).astype(BF16), w2v_ref[...])

    q = jnp.concatenate([q_ref[:, h * B_DK:(h + 1) * B_DK] for h in range(B_HEADS)], axis=0)
    qb = (q * (SCALE_B * LOG2E)).astype(BF16)
    qpos = past + _mod(lax.broadcasted_iota(jnp.int32, (rows, 1), 0), ts)
    c_end = lax.broadcasted_iota(jnp.int32, (rows, n_out), 1) * CMP_STRIDE + (CMP_LEN - 1)
    c_ok = c_end <= qpos
    s = jnp.where(c_ok, _nt(qb, kc.astype(BF16)), NEG)
    m = jnp.max(s, axis=-1, keepdims=True)
    e = jnp.where(c_ok, jnp.exp2(s - m), 0.0)
    pr = e / jnp.maximum(jnp.sum(e, axis=-1, keepdims=True), 1e-30)
    ocmp_ref[...] = _nn(pr.astype(BF16), vc.astype(BF16))
    imp = _nn_exact_lhs01(gsum_ref[...], _nn_exact_rhs01(pr, cover_ref[...]))
    score = _block_scores(imp, qpos)
    n_s = (past + ts + SEL_BLOCK - 1) // SEL_BLOCK
    n_sp = -(-n_s // 8) * 8
    sel_t = _topk_mask(score.T[0:n_sp, :], min(N_SELECT, n_s))
    sel = jnp.concatenate([sel_t, jnp.zeros((imp.shape[1] - n_sp, rows), F32)], axis=0).T
    sel_ref[...] = sel[0:ts, :]


def _cmp_sample(page_table, proj, cache_t, cw, perm, cover, gsum, layer, row0, nb, ts):
    n_pages = page_table.shape[1]
    past = n_pages * PAGE
    rb0 = row0 // ts
    n_out = -(-(past + ts) // CMP_STRIDE) - N_HALF + 1
    n_ch = past // CMP_STRIDE + 2 * PAGE // CMP_STRIDE
    rows = B_HEADS * ts
    n_slots = cover.shape[1]
    const = lambda shape: pl.BlockSpec(shape, lambda b, pt: (0,) * len(shape))
    grid_spec = pltpu.PrefetchScalarGridSpec(
        num_scalar_prefetch=1,
        grid=(nb,),
        in_specs=[pl.BlockSpec((ts, B_HEADS * B_DK), lambda b, pt: (rb0 + b, OFF_BQ // (B_HEADS * B_DK))),
                  pl.BlockSpec((ts, 1024), lambda b, pt: (rb0 + b, OFF_BM // 1024)),
                  pl.BlockSpec(memory_space=pl.ANY),
                  const((CMP_LEN, 128)), const((CMP_LEN, 128)), const((CMP_LEN, 128)),
                  const((8, 256, 256)), const((4, 256, 256)),
                  const((128, 128)), const((128, 128)),
                  const((2 * PAGE, 2 * PAGE)), const((n_out, n_slots)), const((rows, rows))],
        out_specs=[pl.BlockSpec((None, rows, 128), lambda b, pt: (b, 0, 0)),
                   pl.BlockSpec((None, ts, n_slots), lambda b, pt: (b, 0, 0))],
        scratch_shapes=[pltpu.VMEM((2, ROW_B, past), F32),
                        pltpu.VMEM((8, n_ch, 256), BF16), pltpu.VMEM((4, n_ch, 256), BF16),
                        pltpu.VMEM((n_ch, 256), F32), pltpu.VMEM((n_ch, 256), F32),
                        pltpu.VMEM((8, 128), F32), pltpu.SemaphoreType.DMA((2,))],
    )
    return pl.pallas_call(
        functools.partial(_cmp_sample_kernel, layer=layer, n_pages=n_pages, ts=ts, n_out=n_out, past=past),
        grid_spec=grid_spec,
        out_shape=[jax.ShapeDtypeStruct((nb, rows, 128), F32),
                   jax.ShapeDtypeStruct((nb, ts, n_slots), F32)],
        compiler_params=_params("arbitrary"),
        name="cmp_sample",
    )(page_table, proj, proj, cache_t, cw["pek"], cw["pev_lo"], cw["pev_hi"], cw["wk2"], cw["wv4"], cw["w2k"], cw["w2v"],
      perm, cover, gsum)


def _slc_sample_kernel(pt_ref, q_ref, bm_ref, sel_ref, ocmp_ref, win_ref, eneg_ref, et_ref, cache_ref,
                       o_ref, nwin_ref, sbuf_ref, tail_ref, wscr_ref, opad_ref, sem, *, layer, n_pages, ts, chunk, past):
    slot = _gather_step(pt_ref, cache_ref, layer, n_pages, sbuf_ref, sem, lambda: None)
    rows = B_HEADS * ts
    wb = win_ref.shape[0]
    new_slc = bm_ref[:, BM_SLC:BM_SLC + SLOT]
    new_win = bm_ref[:, BM_WIN:BM_WIN + SLOT]
    q = jnp.concatenate([q_ref[:, h * B_DK:(h + 1) * B_DK] for h in range(B_HEADS)], axis=0)
    qb = (q * (SCALE_B * LOG2E)).astype(BF16)
    tq = _mod(lax.broadcasted_iota(jnp.int32, (rows, 1), 0), ts)
    sel = jnp.concatenate([sel_ref[...]] * B_HEADS, axis=0)
    unsel = (1.0 - sel).astype(BF16)

    carry = (jnp.full((rows, 1), NEG, F32), jnp.zeros((rows, 1), F32), jnp.zeros((rows, B_DV), F32))
    for c in range(past // chunk):
        kt = sbuf_ref[slot, 0:B_DK, c * chunk:(c + 1) * chunk].astype(BF16)
        vt = sbuf_ref[slot, B_DK:ROW_B, c * chunk:(c + 1) * chunk].astype(BF16)
        s = _nn(qb, kt) + _nn(unsel[:, 0:128], eneg_ref[c])
        carry = _online_chunk(s, lambda p: _nt(p, vt), carry)
    tail_ref[...] = jnp.zeros_like(tail_ref)
    tail_ref[0:ts, :] = new_slc
    kn = tail_ref[...].astype(BF16)
    tk = lax.broadcasted_iota(jnp.int32, (rows, PAGE), 1)
    in_sel = _nn(sel[:, 128:256].astype(BF16), et_ref[...])
    bias = jnp.where((in_sel > 0.5) & (tk <= tq), 0.0, NEG)
    _, l, acc = _online_chunk(_nt(qb, kn[:, 0:B_DK]) + bias, lambda p: _nn(p, kn[:, B_DK:ROW_B]), carry)
    osel = acc / l

    wscr_ref[...] = jnp.zeros_like(wscr_ref)
    wscr_ref[0:wb, 0:B_DK] = win_ref[:, 0:B_DK]
    wscr_ref[0:wb, B_DK:ROW_B] = win_ref[:, B_DK:ROW_B]
    wscr_ref[wb:wb + ts, :] = new_win
    kw = wscr_ref[...].astype(BF16)
    wpos = past - wb + lax.broadcasted_iota(jnp.int32, (rows, wscr_ref.shape[0]), 1)
    qpos = past + tq
    bias_w = jnp.where((wpos <= qpos) & (wpos >= qpos - WINDOW) & (wpos < past + ts), 0.0, NEG)
    s = _nt(qb, kw[:, 0:B_DK]) + bias_w
    e = jnp.exp2(s - jnp.max(s, axis=-1, keepdims=True))
    owin = _nn(e.astype(BF16), kw[:, B_DK:ROW_B]) / jnp.sum(e, axis=-1, keepdims=True)

    gs = jnp.concatenate([jax.nn.sigmoid(bm_ref[:, BM_G:BM_G + 128])] * B_HEADS, axis=0)
    col = lax.broadcasted_iota(jnp.int32, (rows, 128), 1)
    hd = _div(lax.broadcasted_iota(jnp.int32, (rows, 128), 0), ts)

    def gate(j):
        return jnp.sum(jnp.where(col == 3 * hd + j, gs, 0.0), axis=-1, keepdims=True)

    opad_ref[...] = jnp.zeros_like(opad_ref)
    opad_ref[:, 0:B_DV] = gate(0) * ocmp_ref[:, 0:B_DV] + gate(1) * osel + gate(2) * owin
    o = opad_ref[...]
    for p in range(B_HEADS // 2):
        lo = o[(2 * p) * ts:(2 * p + 1) * ts, :]
        hi = pltpu.roll(o[(2 * p + 1) * ts:(2 * p + 2) * ts, :], 64, 1)
        o_ref[:, p * 128:(p + 1) * 128] = lo + hi

    nwin_ref[0:wb - ts, :] = win_ref[ts:wb, :]
    nwin_ref[wb - ts:wb, :] = new_win[:, 0:ROW_B]


def _slc_sample(page_table, proj, sel, ocmp, state_win, eneg, etail, cache_t, layer, row0, nb, ts, chunk=2048):
    n_pages = page_table.shape[1]
    past = n_pages * PAGE
    rb0 = row0 // ts
    rows = B_HEADS * ts
    wb = state_win.shape[2]
    n_slots = sel.shape[2]
    wrows = -(-(wb + ts) // 128) * 128
    grid_spec = pltpu.PrefetchScalarGridSpec(
        num_scalar_prefetch=1,
        grid=(nb,),
        in_specs=[pl.BlockSpec((ts, B_HEADS * B_DK), lambda b, pt: (rb0 + b, OFF_BQ // (B_HEADS * B_DK))),
                  pl.BlockSpec((ts, 1024), lambda b, pt: (rb0 + b, OFF_BM // 1024)),
                  pl.BlockSpec((None, ts, n_slots), lambda b, pt: (b, 0, 0)),
                  pl.BlockSpec((None, rows, 128), lambda b, pt: (b, 0, 0)),
                  pl.BlockSpec((None, None, wb, ROW_B), lambda b, pt: (b, layer, 0, 0)),
                  pl.BlockSpec(eneg.shape, lambda b, pt: (0, 0, 0)),
                  pl.BlockSpec(etail.shape, lambda b, pt: (0, 0)),
                  pl.BlockSpec(memory_space=pl.ANY)],
        out_specs=[pl.BlockSpec((ts, B_HEADS * B_DV), lambda b, pt: (b, 0)),
                   pl.BlockSpec((None, wb, ROW_B), lambda b, pt: (b, 0, 0))],
        scratch_shapes=[pltpu.VMEM((2, ROW_B, past), F32), pltpu.VMEM((PAGE, SLOT), F32),
                        pltpu.VMEM((wrows, SLOT), F32), pltpu.VMEM((rows, 128), F32),
                        pltpu.SemaphoreType.DMA((2,))],
    )
    return pl.pallas_call(
        functools.partial(_slc_sample_kernel, layer=layer, n_pages=n_pages, ts=ts, chunk=chunk, past=past),
        grid_spec=grid_spec,
        out_shape=[jax.ShapeDtypeStruct((nb * ts, B_HEADS * B_DV), F32),
                   jax.ShapeDtypeStruct((nb, wb, ROW_B), F32)],
        compiler_params=_params("arbitrary"),
        name="slc_sample",
    )(page_table, proj, proj, sel, ocmp, state_win, eneg, etail, cache_t)


def _merge_kernel(yap_ref, yas_ref, obp_ref, obs_ref, olp_ref, ols_ref, bz_ref, cz_ref, gates_ref,
                  wuv_ref, wpa_ref, wpb_ref, wpc_ref, o_ref, *, n_prompt):
    is_p = pl.program_id(0) < n_prompt
    ya = jnp.where(is_p, yap_ref[...], yas_ref[...])
    ob = jnp.where(is_p, obp_ref[...], obs_ref[...])
    yb = (ob * _silu(bz_ref[...])).astype(BF16)
    acc = jax.nn.sigmoid(gates_ref[:, 0:D_MODEL]) * _nn(ya, wpa_ref[...])
    acc = acc + jax.nn.sigmoid(gates_ref[:, D_MODEL:2 * D_MODEL]) * _nn(yb, wpb_ref[...])
    yc = jnp.concatenate([_nn(jnp.where(is_p, olp_ref[h], ols_ref[h]).astype(BF16), wuv_ref[h])
                          for h in range(C_HEADS)], axis=1)
    yc = (yc * _silu(cz_ref[...])).astype(BF16)
    acc = acc + jax.nn.sigmoid(gates_ref[:, 2 * D_MODEL:3 * D_MODEL]) * _nn(yc, wpc_ref[...])
    o_ref[...] = acc.astype(BF16)


def _merge(ya_p, ya_s, ob_p, ob_s, olat_p, olat_s, proj, wuv, wpa, wpb, wpc, tm=256):
    m = proj.shape[0]
    n_p = ya_p.shape[0] // tm
    n_s = ya_s.shape[0] // tm
    rp = lambda i: jnp.minimum(i, n_p - 1)
    rs = lambda i: jnp.clip(i - n_p, 0, n_s - 1)
    const = lambda shape: pl.BlockSpec(shape, lambda i: (0,) * len(shape))
    return pl.pallas_call(
        functools.partial(_merge_kernel, n_prompt=n_p),
        grid=(m // tm,),
        in_specs=[pl.BlockSpec((tm, A_WIDTH), lambda i: (rp(i), 0)),
                  pl.BlockSpec((tm, A_WIDTH), lambda i: (rs(i), 0)),
                  pl.BlockSpec((tm, B_HEADS * B_DV), lambda i: (rp(i), 0)),
                  pl.BlockSpec((tm, B_HEADS * B_DV), lambda i: (rs(i), 0)),
                  pl.BlockSpec((C_HEADS, tm, C_KV_LORA), lambda i: (0, rp(i), 0)),
                  pl.BlockSpec((C_HEADS, tm, C_KV_LORA), lambda i: (0, rs(i), 0)),
                  pl.BlockSpec((tm, 1024), lambda i: (i, OFF_BZ // 1024)),
                  pl.BlockSpec((tm, 1024), lambda i: (i, OFF_CZ // 1024)),
                  pl.BlockSpec((tm, 3 * D_MODEL), lambda i: (i, OFF_GATES // (3 * D_MODEL))),
                  const((C_HEADS, C_KV_LORA, 128)), const((A_WIDTH, D_MODEL)),
                  const((B_HEADS * B_DV, D_MODEL)), const((C_HEADS * 128, D_MODEL))],
        out_specs=pl.BlockSpec((tm, D_MODEL), lambda i: (i, 0)),
        out_shape=jax.ShapeDtypeStruct((m, D_MODEL), BF16),
        compiler_params=_params("arbitrary"),
        name="merge",
    )(ya_p, ya_s, ob_p, ob_s, olat_p, olat_s, proj, proj, proj, wuv, wpa, wpb, wpc)


def _outproj_kernel(mg_ref, x_ref, w_ref, g_ref, o_ref):
    out = _nn(mg_ref[...], w_ref[...])
    o_ref[...] = x_ref[...] + _rms(out, g_ref[...])


def _outproj(merged, x, w, g, tm=512):
    m = x.shape[0]
    return pl.pallas_call(
        _outproj_kernel,
        grid=(m // tm,),
        in_specs=[pl.BlockSpec((tm, D_MODEL), lambda i: (i, 0)),
                  pl.BlockSpec((tm, D_MODEL), lambda i: (i, 0)),
                  pl.BlockSpec((D_MODEL, D_MODEL), lambda i: (0, 0)),
                  pl.BlockSpec((1, D_MODEL), lambda i: (0, 0))],
        out_specs=pl.BlockSpec((tm, D_MODEL), lambda i: (i, 0)),
        out_shape=jax.ShapeDtypeStruct((m, D_MODEL), F32),
        compiler_params=_params("arbitrary"),
        name="out_proj",
    )(merged, x, w, g)


def _pack_w_in(w_in):
    (a_uv, a_z, b_q, b_kc, b_vc, b_ks, b_vs, b_kw, b_vw, b_g, b_z, c_dq, c_dkv, c_z, gates) = jnp.split(
        w_in, SPLIT_AT, axis=-1)
    z = lambda n: jnp.zeros(w_in.shape[:-1] + (n,), w_in.dtype)
    kr = c_dkv[..., C_KV_LORA:]
    kr_sw = jnp.concatenate([kr[..., C_ROPE // 2:], kr[..., :C_ROPE // 2]], axis=-1)
    bm = jnp.concatenate([b_kc, b_vc, z(64), b_ks, b_vs, z(64), b_kw, b_vw, z(64), b_g, z(1024 - 768 - 48)], axis=-1)
    cm = jnp.concatenate([c_dq, c_dkv[..., :C_KV_LORA], kr, kr_sw, z(1024 - 768)], axis=-1)
    return jnp.concatenate([gates, a_uv, b_q, a_z, bm, b_z, cm, c_z], axis=-1).astype(BF16)


def _pack_w_uq(w_uq):
    d = w_uq.shape[0]
    w = w_uq.reshape(d, C_Q_LORA, C_HEADS, C_NOPE + C_ROPE)
    nope = w[..., :C_NOPE].reshape(d, C_Q_LORA, C_HEADS * C_NOPE)
    r = w[..., C_NOPE:]
    r_sw = jnp.concatenate([r[..., C_ROPE // 2:], r[..., :C_ROPE // 2]], axis=-1)
    rope = jnp.concatenate([r, r_sw], axis=-1).reshape(d, C_Q_LORA, C_HEADS * 128)
    return jnp.concatenate([nope, rope], axis=-1).astype(BF16)


def _pack_cmp(pe_k, w1_k, w2_k, pe_v, w1_v, w2_v):
    d = pe_k.shape[0]
    wk = jnp.concatenate([w1_k[:, :CMP_STRIDE], w1_k[:, CMP_STRIDE:]], axis=-1)
    zv = jnp.zeros((d, CMP_STRIDE, B_DV, 128 - B_DV), F32)
    wv = jnp.concatenate([w1_v[:, :CMP_STRIDE], zv, w1_v[:, CMP_STRIDE:], zv], axis=-1)
    wk2 = wk.reshape(d, 2, 8, B_DK, 256).transpose(0, 2, 1, 3, 4).reshape(d, 8, 2 * B_DK, 256)
    wv4 = wv.reshape(d, 4, 4, B_DV, 256).transpose(0, 2, 1, 3, 4).reshape(d, 4, 4 * B_DV, 256)
    pad_hi = ((0, 0), (0, 0), (0, 128 - B_DV))
    pad_lo = ((0, 0), (0, 0), (128 - B_DV, 0))
    return dict(pek=pe_k, pev=jnp.pad(pe_v, pad_hi), pev_lo=jnp.pad(pe_v, pad_hi), pev_hi=jnp.pad(pe_v, pad_lo),
                wk=wk.astype(BF16), wv=jnp.pad(wv, ((0, 0), (0, 0), (0, 128 - B_DV), (0, 0))).astype(BF16),
                wk2=wk2.astype(BF16), wv4=wv4.astype(BF16), w2k=w2_k.astype(BF16),
                w2v=jnp.pad(w2_v, ((0, 0), (0, 128 - B_DV), (0, 128 - B_DV))).astype(BF16))


def _rope_table(pos):
    half = C_ROPE // 2
    inv = ROPE_THETA ** (-jnp.arange(half, dtype=F32) / half)
    ang = pos.astype(F32)[:, None] * inv
    cos, sin = jnp.cos(ang), jnp.sin(ang)
    return jnp.concatenate([cos, cos, -sin, sin], axis=-1)


def _cover(n_c, n_c_pad, n_s, n_s_pad):
    ci = np.arange(n_c)[:, None]
    sj = np.arange(n_s)[None, :]
    cov = ((ci * CMP_STRIDE < (sj + 1) * SEL_BLOCK) & (ci * CMP_STRIDE + CMP_LEN > sj * SEL_BLOCK)).astype(np.float32)
    out = np.zeros((n_c_pad, n_s_pad), np.float32)
    out[:n_c, :n_s] = cov
    return jnp.asarray(out, BF16)


def _expand(n_slots, key0, n_keys, slot0=0):
    j = np.arange(n_slots)[:, None] + slot0
    k = (np.arange(n_keys)[None, :] + key0) // SEL_BLOCK
    return (j == k).astype(np.float32)


def kernel(x_prompt, x_sample, cache_mla, cache_nsa_cmp, cache_nsa_slc, state_nsa_win, page_table, pre_norm_g, w_in, a_ln_g, a_ln_b, a_ws, a_bs, b_cmp_pe_k, b_cmp_w1_k, b_cmp_w2_k, b_cmp_pe_v, b_cmp_w1_v, b_cmp_w2_v, c_q_norm_g, c_kv_norm_g, c_w_uq, c_w_uk, c_w_uv, w_proj_a, w_proj_b, w_proj_c, w_out, post_norm_g):
    nbp, tp, _ = x_prompt.shape
    nbs, ts, _ = x_sample.shape
    n_pages = page_table.shape[1]
    past = n_pages * PAGE
    mp = nbp * tp
    ms = nbs * ts
    depth = w_in.shape[0]

    w_in_p = _pack_w_in(w_in)
    w_uq_p = _pack_w_uq(c_w_uq)
    w_ukt = jnp.transpose(c_w_uk, (0, 2, 3, 1)).astype(BF16)
    w_uv = jnp.transpose(c_w_uv, (0, 2, 1, 3)).astype(BF16)
    cw = _pack_cmp(b_cmp_pe_k, b_cmp_w1_k, b_cmp_w2_k, b_cmp_pe_v, b_cmp_w1_v, b_cmp_w2_v)
    wpa, wpb, wpc, wo = (w.astype(BF16) for w in (w_proj_a, w_proj_b, w_proj_c, w_out))
    tab = jnp.concatenate([jnp.tile(_rope_table(jnp.arange(tp)), (nbp, 1)),
                           jnp.tile(_rope_table(past + jnp.arange(ts)), (nbs, 1))], axis=0)
    bs_p = jnp.repeat(jnp.transpose(a_bs, (0, 2, 1)), A_WIDTH // A_GROUPS, axis=-1)
    reps = A_CHUNK // ts
    ws_s = jnp.tile(a_ws[:, :, :ts, :ts], (1, 1, reps, reps))
    bs_s = jnp.tile(bs_p[:, :ts], (1, reps, 1))

    ncp = tp // CMP_STRIDE
    cover_p = _cover(ncp - N_HALF + 1, ncp, tp // SEL_BLOCK, 128)
    eneg_p = jnp.asarray(_expand(128, 0, tp).T * NEG, BF16)
    gsel = np.zeros((128, 3 * (B_HEADS // 2), 128), np.float32)
    for p in range(B_HEADS // 2):
        for j in range(3):
            gsel[3 * (2 * p) + j, 3 * p + j, :B_DV] = 1.0
            gsel[3 * (2 * p + 1) + j, 3 * p + j, B_DV:] = 1.0
    gsel = jnp.asarray(gsel.reshape(128, -1), BF16)
    ncs = -(-(past + ts) // CMP_STRIDE) - N_HALF + 1
    nss = -(-(past + ts) // SEL_BLOCK)
    cover_s = _cover(ncs, ncs, nss, 256)
    chunk = 2048
    eneg_s = jnp.asarray(np.stack([_expand(128, c * chunk, chunk) for c in range(past // chunk)]) * NEG, BF16)
    etail_s = jnp.asarray(_expand(128, past, PAGE, slot0=128), BF16)
    rr = np.arange(B_HEADS * ts)
    gsum = jnp.asarray((rr[:, None] % ts == rr[None, :] % ts).astype(np.float32), BF16)
    cache_mla_t = jnp.swapaxes(cache_mla, 2, 3)
    cache_slc_t = jnp.swapaxes(cache_nsa_slc, 2, 3)
    cache_cmp_t = jnp.swapaxes(cache_nsa_cmp, 2, 3)
    grp = 2 * PAGE
    pos = np.arange(grp)
    perm = np.zeros((grp, grp), np.float32)
    perm[(pos % CMP_STRIDE) * (grp // CMP_STRIDE) + pos // CMP_STRIDE, pos] = 1.0
    perm = jnp.asarray(perm, BF16)

    x = jnp.concatenate([x_prompt.reshape(mp, D_MODEL), x_sample.reshape(ms, D_MODEL)], axis=0)
    outs = {k: [] for k in ("mla_p", "mla_s", "cmp_p", "cmp_s", "slc_p", "slc_s", "win_p", "win_s", "av_s")}
    for l in range(depth):
        cwl = {k: v[l] for k, v in cw.items()}
        proj = _inproj(x, pre_norm_g[l][None], w_in_p[l])
        ya_p = _mixa(proj, 0, mp, a_ln_g[l][None], a_ln_b[l][None], a_ws[l], bs_p[l], sample=False)[0]
        ya_s, vn_s = _mixa(proj, mp, ms, a_ln_g[l][None], a_ln_b[l][None], ws_s[l], bs_s[l], sample=True)
        qcat, ckr = _mlaprep(proj, tab, c_q_norm_g[l][None], c_kv_norm_g[l][None], w_uq_p[l], w_ukt[l])
        kc_p, vc_p = _cmp_prompt(proj, nbp, tp, cwl)
        ob_p = _nsa_prompt(proj, kc_p, vc_p, cover_p, eneg_p, gsel, nbp, tp)
        olat_p = _mla_prompt(qcat, ckr, nbp, tp)
        olat_s = _mla_sample(page_table, qcat, ckr, cache_mla_t, l, mp, nbs, ts, chunk)
        ocmp_s, sel_s = _cmp_sample(page_table, proj, cache_cmp_t, cwl, perm, cover_s, gsum, l, mp, nbs, ts)
        ob_s, nwin_s = _slc_sample(page_table, proj, sel_s, ocmp_s, state_nsa_win, eneg_s, etail_s,
                                   cache_slc_t, l, mp, nbs, ts, chunk)
        merged = _merge(ya_p, ya_s, ob_p, ob_s, olat_p, olat_s, proj, w_uv[l], wpa[l], wpb[l], wpc[l])
        x = _outproj(merged, x, wo[l], post_norm_g[l][None])

        rows = lambda off: proj[:, OFF_BM + off:OFF_BM + off + ROW_B]
        cmp_r, slc_r, win_r = rows(BM_CMP), rows(BM_SLC), rows(BM_WIN)
        outs["mla_p"].append(ckr[:mp].reshape(nbp, tp, ROW_C))
        outs["mla_s"].append(ckr[mp:].reshape(nbs, ts, ROW_C))
        outs["cmp_p"].append(cmp_r[:mp].reshape(nbp, tp, ROW_B))
        outs["cmp_s"].append(cmp_r[mp:].reshape(nbs, ts, ROW_B))
        outs["slc_p"].append(slc_r[:mp].reshape(nbp, tp, ROW_B))
        outs["slc_s"].append(slc_r[mp:].reshape(nbs, ts, ROW_B))
        outs["win_p"].append(win_r[:mp].reshape(nbp, tp, ROW_B)[:, -min(WINDOW, tp):])
        outs["win_s"].append(nwin_s)
        outs["av_s"].append(vn_s.reshape(nbs, ts, A_WIDTH))

    st = lambda k: jnp.stack(outs[k], axis=1)
    return (x[:mp].reshape(nbp, tp, D_MODEL), x[mp:].reshape(nbs, ts, D_MODEL),
            st("mla_p"), st("mla_s"), st("cmp_p"), st("cmp_s"), st("slc_p"), st("slc_s"),
            st("win_p"), st("win_s"), st("av_s"))
```
